```python
import jax, jax.numpy as jnp
from jax import lax
import numpy as np

D_MODEL = 1024
BATCH = 8
SEQ = 4096
DEPTH = 2

PLE_DIM = 256
SSD_HEADS = 16
SSD_HEAD_DIM = 64
SSD_INNER = SSD_HEADS * SSD_HEAD_DIM
SSD_GROUPS = 4
SSD_STATE = 128
SSD_CONV = 4
SSD_CHUNK = 128
SSD_CONV_DIM = SSD_INNER + 2 * SSD_GROUPS * SSD_STATE
CM_DIM = D_MODEL
CM_KERNEL = 31
FFN_DIM = 2816
FFN_CONV = 3
N_BRANCHES = 2
IN_COLS = SSD_INNER + SSD_CONV_DIM + SSD_HEADS + 2 * CM_DIM + N_BRANCHES * D_MODEL
SPLITS = (SSD_INNER,
          SSD_INNER + SSD_CONV_DIM,
          SSD_INNER + SSD_CONV_DIM + SSD_HEADS,
          SSD_INNER + SSD_CONV_DIM + SSD_HEADS + 2 * CM_DIM)
EPS = 1e-6

kernel_name = 'hybrid_ssd_conformer_gated_trunk'


def rmsnorm(x, g):
    xf = x.astype(jnp.float32)
    y = xf * lax.rsqrt(jnp.mean(xf * xf, axis=-1, keepdims=True) + EPS)
    return (y * g.astype(jnp.float32)).astype(x.dtype)


def layernorm(x, g, b):
    xf = x.astype(jnp.float32)
    mu = jnp.mean(xf, axis=-1, keepdims=True)
    var = jnp.mean(jnp.square(xf - mu), axis=-1, keepdims=True)
    y = (xf - mu) * lax.rsqrt(var + EPS)
    return (y * g.astype(jnp.float32) + b.astype(jnp.float32)).astype(x.dtype)


def causal_dwconv(u, w, b):
    k, c = w.shape
    y = lax.conv_general_dilated(u, w[:, None, :].astype(u.dtype), window_strides=(1,),
                                 padding=[(k - 1, 0)], dimension_numbers=('NWC', 'WIO', 'NWC'),
                                 feature_group_count=c)
    return y + b.astype(u.dtype)


def ssd_mixer(z, xbc, dt_raw, conv_w, conv_b, dt_bias, a_log, d_skip, norm_w):
    f32 = jnp.float32
    bsz, seq, _ = z.shape
    nc = seq // SSD_CHUNK
    hpg = SSD_HEADS // SSD_GROUPS
    xbc = jax.nn.silu(causal_dwconv(xbc, conv_w, conv_b)).astype(f32)
    xs, bm, cm = jnp.split(xbc, [SSD_INNER, SSD_INNER + SSD_GROUPS * SSD_STATE], axis=-1)
    dt = jax.nn.softplus(dt_raw.astype(f32) + dt_bias.astype(f32))
    a = -jnp.exp(a_log.astype(f32)).reshape(SSD_GROUPS, hpg)
    shp = (bsz, nc, SSD_CHUNK, SSD_GROUPS, hpg)
    dt = dt.reshape(shp)
    xs = xs.reshape(shp + (SSD_HEAD_DIM,))
    bm = bm.reshape(bsz, nc, SSD_CHUNK, SSD_GROUPS, SSD_STATE)
    cm = cm.reshape(bsz, nc, SSD_CHUNK, SSD_GROUPS, SSD_STATE)
    xdt = xs * dt[..., None]
    a_cum = jnp.cumsum(dt * a, axis=2)
    a_t = jnp.transpose(a_cum, (0, 1, 3, 4, 2))
    seg = a_t[..., :, None] - a_t[..., None, :]
    causal = jnp.tril(jnp.ones((SSD_CHUNK, SSD_CHUNK), dtype=bool))
    decay_ls = jnp.exp(jnp.where(causal, seg, -jnp.inf))
    cb = jnp.einsum('bclgn,bcsgn->bcgls', cm, bm)
    y_diag = jnp.einsum('bcghls,bcsghp->bclghp', cb[:, :, :, None] * decay_ls, xdt)
    decay_to_end = jnp.exp(a_cum[:, :, -1:] - a_cum)
    states = jnp.einsum('bclgn,bclghp->bcghpn', bm, xdt * decay_to_end[..., None])
    chunk_decay = jnp.exp(a_cum[:, :, -1])

    def step(carry, inp):
        st, dec = inp
        return carry * dec[..., None, None] + st, carry

    init = jnp.zeros_like(states[:, 0])
    _, prev = lax.scan(step, init, (jnp.moveaxis(states, 1, 0), jnp.moveaxis(chunk_decay, 1, 0)))
    prev = jnp.moveaxis(prev, 0, 1)
    y_off = jnp.einsum('bclgn,bcghpn->bclghp', cm, prev) * jnp.exp(a_cum)[..., None]
    y = y_diag + y_off + xs * d_skip.astype(f32).reshape(SSD_GROUPS, hpg)[:, :, None]
    y = y.reshape(bsz, seq, SSD_INNER) * jax.nn.silu(z.astype(f32))
    yg = y.reshape(bsz, seq, SSD_GROUPS, SSD_INNER // SSD_GROUPS)
    yg = yg * lax.rsqrt(jnp.mean(yg * yg, axis=-1, keepdims=True) + EPS)
    y = yg.reshape(bsz, seq, SSD_INNER) * norm_w.astype(f32)
    return y.astype(z.dtype)


def conformer_conv(u, conv_w, conv_b, ln_g, ln_b):
    val, gate = jnp.split(u, 2, axis=-1)
    v = val * jax.nn.sigmoid(gate)
    v = causal_dwconv(v, conv_w, conv_b)
    v = layernorm(v, ln_g, ln_b)
    return jax.nn.silu(v)


def setup_inputs(seed: int = 0) -> dict:
    key = jax.random.key(seed)
    ks = jax.random.split(key, 28)
    L, D = DEPTH, D_MODEL
    nrm = lambda k, shape, scale: jax.random.normal(k, shape, jnp.float32) * scale
    u = jax.random.uniform(ks[5], (L, SSD_HEADS), jnp.float32)
    dt0 = jnp.exp(u * (np.log(0.1) - np.log(0.001)) + np.log(0.001)).astype(jnp.float32)
    dt_bias = dt0 + jnp.log(-jnp.expm1(-dt0))
    a_log = jnp.log(jax.random.uniform(ks[6], (L, SSD_HEADS), jnp.float32, 1.0, 16.0))
    return {
        'x': nrm(ks[0], (BATCH, SEQ, D), 1.0),
        'p': nrm(ks[1], (DEPTH, BATCH, SEQ, PLE_DIM), 1.0),
        'norm_mix': 1.0 + nrm(ks[2], (L, D), 0.02),
        'w_in': nrm(ks[3], (L, D, IN_COLS), D ** -0.5),
        'b_gate': nrm(ks[4], (L, N_BRANCHES, D), 0.02),
        'ssd_conv_w': nrm(ks[7], (L, SSD_CONV, SSD_CONV_DIM), SSD_CONV ** -0.5),
        'ssd_conv_b': nrm(ks[8], (L, SSD_CONV_DIM), 0.02),
        'dt_bias': dt_bias,
        'a_log': a_log,
        'd_skip': 1.0 + nrm(ks[9], (L, SSD_HEADS), 0.02),
        'ssd_norm': 1.0 + nrm(ks[10], (L, SSD_INNER), 0.02),
        'w_ssd_out': nrm(ks[11], (L, SSD_INNER, D), SSD_INNER ** -0.5),
        'cm_conv_w': nrm(ks[12], (L, CM_KERNEL, CM_DIM), CM_KERNEL ** -0.5),
        'cm_conv_b': nrm(ks[13], (L, CM_DIM), 0.02),
        'cm_ln_g': 1.0 + nrm(ks[14], (L, CM_DIM), 0.02),
        'cm_ln_b': nrm(ks[15], (L, CM_DIM), 0.02),
        'w_cm_out': nrm(ks[16], (L, CM_DIM, D), CM_DIM ** -0.5),
        'w_out': nrm(ks[17], (L, D, D), D ** -0.5),
        'norm_ffn': 1.0 + nrm(ks[18], (L, D), 0.02),
        'w_up': nrm(ks[19], (L, D, 2 * FFN_DIM), D ** -0.5),
        'ffn_conv_w': nrm(ks[20], (L, FFN_CONV, 2 * FFN_DIM), FFN_CONV ** -0.5),
        'ffn_conv_b': nrm(ks[21], (L, 2 * FFN_DIM), 0.02),
        'w_down': nrm(ks[22], (L, FFN_DIM, D), FFN_DIM ** -0.5),
        'norm_ple': 1.0 + nrm(ks[23], (L, D), 0.02),
        'w_ple_gate': nrm(ks[24], (L, D, D), D ** -0.5),
        'w_ple_proj': nrm(ks[25], (L, PLE_DIM, D), PLE_DIM ** -0.5),
        'final_norm': 1.0 + nrm(ks[26], (D,), 0.02),
    }


def reference(x, p, norm_mix, w_in, b_gate, ssd_conv_w, ssd_conv_b, dt_bias, a_log, d_skip,
              ssd_norm, w_ssd_out, cm_conv_w, cm_conv_b, cm_ln_g, cm_ln_b, w_cm_out, w_out,
              norm_ffn, w_up, ffn_conv_w, ffn_conv_b, w_down, norm_ple, w_ple_gate, w_ple_proj,
              final_norm):
    bsz, seq, _ = x.shape
    for i in range(DEPTH):
        h = rmsnorm(x, norm_mix[i])
        proj = jnp.einsum('bsd,dc->bsc', h, w_in[i])
        z, xbc, dt_raw, cm_in, gates = jnp.split(proj, SPLITS, axis=-1)
        y_a = ssd_mixer(z, xbc, dt_raw, ssd_conv_w[i], ssd_conv_b[i], dt_bias[i], a_log[i],
                        d_skip[i], ssd_norm[i])
        y_a = jnp.einsum('bse,ed->bsd', y_a, w_ssd_out[i])
        y_b = conformer_conv(cm_in, cm_conv_w[i], cm_conv_b[i], cm_ln_g[i], cm_ln_b[i])
        y_b = jnp.einsum('bse,ed->bsd', y_b, w_cm_out[i])
        g = jax.nn.sigmoid(gates.reshape(bsz, seq, N_BRANCHES, D_MODEL) + b_gate[i])
        merged = g[:, :, 0] * y_a + g[:, :, 1] * y_b
        x = x + jnp.einsum('bsd,de->bse', merged, w_out[i])
        h = rmsnorm(x, norm_ffn[i])
        up = causal_dwconv(jnp.einsum('bsd,df->bsf', h, w_up[i]), ffn_conv_w[i], ffn_conv_b[i])
        gate, val = jnp.split(up, 2, axis=-1)
        x = x + jnp.einsum('bsf,fd->bsd', jax.nn.silu(gate) * val, w_down[i])
        h = rmsnorm(x, norm_ple[i])
        pg = jax.nn.sigmoid(jnp.einsum('bsd,de->bse', h, w_ple_gate[i]))
        x = x + pg * jnp.einsum('bsk,kd->bsd', p[i], w_ple_proj[i])
    return rmsnorm(x, final_norm)
```

```python
import functools

import jax
import jax.numpy as jnp
from jax import lax
from jax.experimental import pallas as pl
from jax.experimental.pallas import tpu as pltpu

F32 = jnp.float32
BF16 = jnp.bfloat16

D_MODEL = 1024
PLE_DIM = 256
SSD_HEADS = 16
SSD_HEAD_DIM = 64
SSD_INNER = SSD_HEADS * SSD_HEAD_DIM
SSD_GROUPS = 4
SSD_STATE = 128
SSD_CONV = 4
SSD_CHUNK = 128
SSD_CONV_DIM = SSD_INNER + 2 * SSD_GROUPS * SSD_STATE
GROUP_WIDTH = SSD_INNER // SSD_GROUPS
CM_DIM = D_MODEL
CM_KERNEL = 31
FFN_DIM = 2816
FFN_CONV = 3
EPS = 1e-6

LANES = 128
SUBLANES = 8
VMEM_LIMIT = 56 * 1024 * 1024

TM_IN = 512
NC_IN = 512
TS_SSD = 512
TS_CM = 256
RB_CM = 64
CM_HALO = 32
TS_FFN = 256
CW_FFN = 256


def _sigmoid(x):
    return jax.nn.sigmoid(x)


def _rms(xf, g):
    ms = jnp.mean(xf * xf, axis=-1, keepdims=True)
    return xf * lax.rsqrt(ms + EPS) * g


def _dot(a, b):
    return jnp.dot(a, b, preferred_element_type=F32)


def _split_bf16(x):
    hi = x.astype(BF16)
    lo = (x - hi.astype(F32)).astype(BF16)
    return hi, lo


def _shift_rows(ext, shift):
    return pltpu.roll(ext, shift, 0)[SUBLANES:, :]


def _const_spec(shape):
    nd = len(shape)
    return pl.BlockSpec(shape, lambda *_: (0,) * nd, pipeline_mode=pl.Buffered(1))


def _inproj_kernel(x_ref, g_ref, w_ref, wdt_ref, bg_ref, dtb_ref,
                   z_ref, xbc_ref, v_ref, gs_ref, dt_ref, h_ref):
    h_ref[...] = _rms(x_ref[...], g_ref[...]).astype(BF16)

    def mm(c0):
        return _dot(h_ref[...], w_ref[:, c0:c0 + NC_IN])

    z0, xbc0 = 0, SSD_INNER
    val0 = xbc0 + SSD_CONV_DIM
    gate0 = val0 + CM_DIM
    gs0 = gate0 + CM_DIM
    for j in range(0, SSD_INNER, NC_IN):
        z_ref[:, j:j + NC_IN] = mm(z0 + j).astype(BF16)
    for j in range(0, SSD_CONV_DIM, NC_IN):
        xbc_ref[:, j:j + NC_IN] = mm(xbc0 + j).astype(BF16)
    for j in range(0, CM_DIM, NC_IN):
        v_ref[:, j:j + NC_IN] = (mm(val0 + j) * _sigmoid(mm(gate0 + j))).astype(BF16)
    for j in range(0, 2 * D_MODEL, NC_IN):
        gs_ref[:, j:j + NC_IN] = _sigmoid(mm(gs0 + j) + bg_ref[:, j:j + NC_IN]).astype(BF16)
    dtr = _dot(h_ref[...], wdt_ref[...]) + dtb_ref[...]
    dt = jnp.maximum(dtr, 0.0) + jnp.log1p(jnp.exp(-jnp.abs(dtr)))
    lane = lax.broadcasted_iota(jnp.int32, dt.shape, 1)
    dt_ref[...] = jnp.where(lane < SSD_HEADS, dt, 0.0)


def _inproj(x2d, g, w_main, w_dt, b_gate, dt_bias):
    ntok = x2d.shape[0]
    ncols = w_main.shape[1]
    row = lambda i: (i, 0)
    return pl.pallas_call(
        _inproj_kernel,
        grid=(ntok // TM_IN,),
        in_specs=[
            pl.BlockSpec((TM_IN, D_MODEL), row),
            _const_spec((1, D_MODEL)),
            _const_spec((D_MODEL, ncols)),
            _const_spec((D_MODEL, LANES)),
            _const_spec((1, 2 * D_MODEL)),
            _const_spec((1, LANES)),
        ],
        out_specs=[
            pl.BlockSpec((TM_IN, SSD_INNER), row),
            pl.BlockSpec((TM_IN, SSD_CONV_DIM), row),
            pl.BlockSpec((TM_IN, CM_DIM), row),
            pl.BlockSpec((TM_IN, 2 * D_MODEL), row),
            pl.BlockSpec((TM_IN, LANES), row),
        ],
        out_shape=[
            jax.ShapeDtypeStruct((ntok, SSD_INNER), BF16),
            jax.ShapeDtypeStruct((ntok, SSD_CONV_DIM), BF16),
            jax.ShapeDtypeStruct((ntok, CM_DIM), BF16),
            jax.ShapeDtypeStruct((ntok, 2 * D_MODEL), BF16),
            jax.ShapeDtypeStruct((ntok, LANES), F32),
        ],
        scratch_shapes=[pltpu.VMEM((TM_IN, D_MODEL), BF16)],
        compiler_params=pltpu.CompilerParams(
            dimension_semantics=("arbitrary",), vmem_limit_bytes=VMEM_LIMIT),
        name="inproj",
    )(x2d, g, w_main, w_dt, b_gate, dt_bias)


def _ssd_kernel(z_ref, xbc_ref, dt_ref, cw_ref, cb_ref, alog_ref, dskip_ref, nw_ref,
                out_ref, state_ref, tail_ref, xcv_ref, xdt_ref, xw_ref, eac_ref):
    L = SSD_CHUNK

    @pl.when(pl.program_id(1) == 0)
    def _():
        state_ref[...] = jnp.zeros_like(state_ref)
        tail_ref[...] = jnp.zeros_like(tail_ref)

    li = lax.broadcasted_iota(jnp.int32, (L, L), 0)
    si = lax.broadcasted_iota(jnp.int32, (L, L), 1)
    causal = si <= li
    tri = causal.astype(BF16)
    hrow = lax.broadcasted_iota(jnp.int32, (LANES, SSD_INNER), 0)
    hcol = lax.broadcasted_iota(jnp.int32, (LANES, SSD_INNER), 1)
    head_of = jnp.right_shift(hcol, SSD_HEAD_DIM.bit_length() - 1)
    expand = (head_of == hrow).astype(BF16)
    low_half = si < SSD_HEAD_DIM
    a_row = -jnp.exp(alog_ref[...])

    def chunk(ci, carry):
        rows = pl.ds(pl.multiple_of(ci * L, L), L)

        cblk = 512
        for c0 in range(0, SSD_CONV_DIM, cblk):
            cols = slice(c0, c0 + cblk)
            u = xbc_ref[rows, cols].astype(F32)
            ext = jnp.concatenate([tail_ref[:, cols], u], axis=0)
            acc = cb_ref[:, cols] + cw_ref[SSD_CONV - 1:SSD_CONV, cols] * u
            for sft in range(1, SSD_CONV):
                k = SSD_CONV - 1 - sft
                acc = acc + cw_ref[k:k + 1, cols] * _shift_rows(ext, sft)
            tail_ref[:, cols] = u[L - SUBLANES:, :]
            xcv_ref[:, cols] = acc * _sigmoid(acc)

        dtc = dt_ref[rows, :]
        da_hi, da_lo = _split_bf16(dtc * a_row)
        acum = _dot(tri, da_hi) + _dot(tri, da_lo)
        acum_t = acum.T
        both_hi, both_lo = _split_bf16(jnp.concatenate([dtc, acum], axis=0))
        ex = _dot(both_hi, expand) + _dot(both_lo, expand)
        dt_exp = ex[:L]
        ac_exp = ex[L:]
        a_last = ac_exp[L - 1:L, :]
        xdt = xcv_ref[:, :SSD_INNER] * dt_exp
        xdt_ref[...] = xdt.astype(BF16)
        xw_ref[...] = (xdt * jnp.exp(a_last - ac_exp)).astype(BF16)
        eac_ref[...] = jnp.exp(ac_exp)
        chunk_decay = jnp.exp(a_last)

        for g in range(SSD_GROUPS):
            gcols = slice(g * GROUP_WIDTH, (g + 1) * GROUP_WIDTH)
            b0 = SSD_INNER + g * SSD_STATE
            c0 = SSD_INNER + SSD_GROUPS * SSD_STATE + g * SSD_STATE
            bm_f = xcv_ref[:, b0:b0 + SSD_STATE]
            bm_b = bm_f.astype(BF16)
            bmt_b = bm_f.T.astype(BF16)
            cm_b = xcv_ref[:, c0:c0 + SSD_STATE].astype(BF16)
            cbm = lax.dot_general(cm_b, bm_b, (((1,), (1,)), ((), ())),
                                  preferred_element_type=F32)
            st = state_ref[g]
            y_off = _dot(cm_b, st.astype(BF16)) * eac_ref[:, gcols]
            state_ref[g] = st * chunk_decay[:, gcols] + _dot(bmt_b, xw_ref[:, gcols])

            y_pairs = []
            for jj in range(2):
                j = 2 * g + jj
                scores = []
                for h in (2 * j, 2 * j + 1):
                    seg = acum[:, h:h + 1] - acum_t[h:h + 1, :]
                    decay = jnp.exp(jnp.where(causal, seg, -jnp.inf))
                    scores.append((cbm * decay).astype(BF16))
                lhs = jnp.concatenate(scores, axis=1)
                xp = xdt_ref[:, j * LANES:(j + 1) * LANES]
                zero = jnp.zeros_like(xp)
                rhs = jnp.concatenate([jnp.where(low_half, xp, zero),
                                       jnp.where(low_half, zero, xp)], axis=0)
                y_pairs.append(_dot(lhs, rhs))
            y = (jnp.concatenate(y_pairs, axis=1) + y_off
                 + xcv_ref[:, gcols] * dskip_ref[:, gcols])
            zf = z_ref[rows, gcols].astype(F32)
            yz = y * (zf * _sigmoid(zf))
            ms = jnp.mean(yz * yz, axis=-1, keepdims=True)
            out_ref[rows, gcols] = (yz * lax.rsqrt(ms + EPS) * nw_ref[:, gcols]).astype(BF16)
        return carry

    lax.fori_loop(0, TS_SSD // L, chunk, 0)


def _ssd(z, xbc, dt, conv_w, conv_b, a_log, d_skip, norm_w, bsz, seq):
    ntok = z.shape[0]
    nt = seq // TS_SSD
    row = lambda b, s: (b * nt + s, 0)
    return pl.pallas_call(
        _ssd_kernel,
        grid=(bsz, nt),
        in_specs=[
            pl.BlockSpec((TS_SSD, SSD_INNER), row),
            pl.BlockSpec((TS_SSD, SSD_CONV_DIM), row),
            pl.BlockSpec((TS_SSD, LANES), row),
            _const_spec((SSD_CONV, SSD_CONV_DIM)),
            _const_spec((1, SSD_CONV_DIM)),
            _const_spec((1, LANES)),
            _const_spec((1, SSD_INNER)),
            _const_spec((1, SSD_INNER)),
        ],
        out_specs=pl.BlockSpec((TS_SSD, SSD_INNER), row),
        out_shape=jax.ShapeDtypeStruct((ntok, SSD_INNER), BF16),
        scratch_shapes=[
            pltpu.VMEM((SSD_GROUPS, SSD_STATE, GROUP_WIDTH), F32),
            pltpu.VMEM((SUBLANES, SSD_CONV_DIM), F32),
            pltpu.VMEM((SSD_CHUNK, SSD_CONV_DIM), F32),
            pltpu.VMEM((SSD_CHUNK, SSD_INNER), BF16),
            pltpu.VMEM((SSD_CHUNK, SSD_INNER), BF16),
            pltpu.VMEM((SSD_CHUNK, SSD_INNER), F32),
        ],
        compiler_params=pltpu.CompilerParams(
            dimension_semantics=("arbitrary", "arbitrary"), vmem_limit_bytes=VMEM_LIMIT),
        name="ssd",
    )(z, xbc, dt, conv_w, conv_b, a_log, d_skip, norm_w)


def _cm_merge_kernel(v_ref, ya_ref, gs_ref, x_ref, cw_ref, cb_ref, lng_ref, lnb_ref,
                     wa_ref, wb_ref, wo_ref, out_ref, vext_ref, conv_ref):
    @pl.when(pl.program_id(1) == 0)
    def _():
        vext_ref[0:CM_HALO, :] = jnp.zeros((CM_HALO, CM_DIM), F32)

    vext_ref[CM_HALO:CM_HALO + TS_CM, :] = v_ref[...].astype(F32)
    win_rows = RB_CM + CM_HALO
    first = CM_HALO - (CM_KERNEL - 1)

    def lane_block(lb, carry):
        lanes = pl.ds(pl.multiple_of(lb * LANES, LANES), LANES)
        for rb in range(TS_CM // RB_CM):
            win = vext_ref[rb * RB_CM:rb * RB_CM + win_rows, lanes]
            acc = jnp.broadcast_to(cb_ref[:, lanes], (RB_CM, LANES))
            for r in range(SUBLANES):
                wr = win if r == 0 else pltpu.roll(win, win_rows - r, 0)
                for q in range(win_rows // SUBLANES):
                    k = SUBLANES * q + r - first
                    if 0 <= k < CM_KERNEL:
                        acc = acc + cw_ref[k:k + 1, lanes] * wr[SUBLANES * q:SUBLANES * q + RB_CM, :]
            conv_ref[rb * RB_CM:(rb + 1) * RB_CM, lanes] = acc
        return carry

    lax.fori_loop(0, CM_DIM // LANES, lane_block, 0)
    vext_ref[0:CM_HALO, :] = vext_ref[TS_CM:TS_CM + CM_HALO, :]

    c = conv_ref[...]
    mu = jnp.mean(c, axis=-1, keepdims=True)
    d = c - mu
    var = jnp.mean(d * d, axis=-1, keepdims=True)
    yb = d * lax.rsqrt(var + EPS) * lng_ref[...] + lnb_ref[...]
    yb = (yb * _sigmoid(yb)).astype(BF16)
    pa = _dot(ya_ref[...], wa_ref[...])
    pb = _dot(yb, wb_ref[...])
    merged = (gs_ref[:, :D_MODEL].astype(F32) * pa + gs_ref[:, D_MODEL:].astype(F32) * pb)
    out_ref[...] = x_ref[...] + _dot(merged.astype(BF16), wo_ref[...])


def _cm_merge(v, ya, gs, x2d, conv_w, conv_b, ln_g, ln_b, w_a, w_b, w_o, bsz, seq):
    ntok = v.shape[0]
    nt = seq // TS_CM
    row = lambda b, s: (b * nt + s, 0)
    return pl.pallas_call(
        _cm_merge_kernel,
        grid=(bsz, nt),
        in_specs=[
            pl.BlockSpec((TS_CM, CM_DIM), row),
            pl.BlockSpec((TS_CM, SSD_INNER), row),
            pl.BlockSpec((TS_CM, 2 * D_MODEL), row),
            pl.BlockSpec((TS_CM, D_MODEL), row),
            _const_spec((CM_KERNEL + 1, CM_DIM)),
            _const_spec((1, CM_DIM)),
            _const_spec((1, CM_DIM)),
            _const_spec((1, CM_DIM)),
            _const_spec((SSD_INNER, D_MODEL)),
            _const_spec((CM_DIM, D_MODEL)),
            _const_spec((D_MODEL, D_MODEL)),
        ],
        out_specs=pl.BlockSpec((TS_CM, D_MODEL), row),
        out_shape=jax.ShapeDtypeStruct((ntok, D_MODEL), F32),
        scratch_shapes=[
            pltpu.VMEM((CM_HALO + TS_CM, CM_DIM), F32),
            pltpu.VMEM((TS_CM, CM_DIM), F32),
        ],
        compiler_params=pltpu.CompilerParams(
            dimension_semantics=("arbitrary", "arbitrary"), vmem_limit_bytes=VMEM_LIMIT),
        name="cm_merge",
    )(v, ya, gs, x2d, conv_w, conv_b, ln_g, ln_b, w_a, w_b, w_o)


def _ffn_kernel(x_ref, p_ref, gf_ref, wup_ref, fcw_ref, fcb_ref, wdn_ref, gp_ref, wpg_ref,
                wpe_ref, gfin_ref, out_ref, h_ref, tail_ref, acc_ref, *, final):
    @pl.when(pl.program_id(1) == 0)
    def _():
        tail_ref[...] = jnp.zeros_like(tail_ref)

    xf = x_ref[...]
    h_ref[...] = _rms(xf, gf_ref[...]).astype(BF16)
    acc_ref[...] = xf
    for c in range(FFN_DIM // CW_FFN):
        halves = []
        for half in range(2):
            cols = slice(half * FFN_DIM + c * CW_FFN, half * FFN_DIM + (c + 1) * CW_FFN)
            u = _dot(h_ref[...], wup_ref[:, cols])
            ext = jnp.concatenate([tail_ref[:, cols], u], axis=0)
            y = fcb_ref[:, cols] + fcw_ref[FFN_CONV - 1:FFN_CONV, cols] * u
            for sft in range(1, FFN_CONV):
                k = FFN_CONV - 1 - sft
                y = y + fcw_ref[k:k + 1, cols] * _shift_rows(ext, sft)
            tail_ref[:, cols] = u[TS_FFN - SUBLANES:, :]
            halves.append(y)
        gate, val = halves
        act = (gate * _sigmoid(gate) * val).astype(BF16)
        acc_ref[...] += _dot(act, wdn_ref[c * CW_FFN:(c + 1) * CW_FFN, :])
    x1 = acc_ref[...]
    h2 = _rms(x1, gp_ref[...]).astype(BF16)
    pg = _sigmoid(_dot(h2, wpg_ref[...]))
    pe = _dot(p_ref[...].astype(BF16), wpe_ref[...])
    x2 = x1 + pg * pe
    if final:
        x2 = _rms(x2, gfin_ref[...])
    out_ref[...] = x2


def _ffn(x2d, p2d, g_ffn, w_up, conv_w, conv_b, w_down, g_ple, w_pg, w_pe, g_final,
         bsz, seq, final):
    ntok = x2d.shape[0]
    nt = seq // TS_FFN
    row = lambda b, s: (b * nt + s, 0)
    return pl.pallas_call(
        functools.partial(_ffn_kernel, final=final),
        grid=(bsz, nt),
        in_specs=[
            pl.BlockSpec((TS_FFN, D_MODEL), row),
            pl.BlockSpec((TS_FFN, PLE_DIM), row),
            _const_spec((1, D_MODEL)),
            _const_spec((D_MODEL, 2 * FFN_DIM)),
            _const_spec((FFN_CONV, 2 * FFN_DIM)),
            _const_spec((1, 2 * FFN_DIM)),
            _const_spec((FFN_DIM, D_MODEL)),
            _const_spec((1, D_MODEL)),
            _const_spec((D_MODEL, D_MODEL)),
            _const_spec((PLE_DIM, D_MODEL)),
            _const_spec((1, D_MODEL)),
        ],
        out_specs=pl.BlockSpec((TS_FFN, D_MODEL), row),
        out_shape=jax.ShapeDtypeStruct((ntok, D_MODEL), F32),
        scratch_shapes=[
            pltpu.VMEM((TS_FFN, D_MODEL), BF16),
            pltpu.VMEM((SUBLANES, 2 * FFN_DIM), F32),
            pltpu.VMEM((TS_FFN, D_MODEL), F32),
        ],
        compiler_params=pltpu.CompilerParams(
            dimension_semantics=("arbitrary", "arbitrary"), vmem_limit_bytes=VMEM_LIMIT),
        name="ffn_ple",
    )(x2d, p2d, g_ffn, w_up, conv_w, conv_b, w_down, g_ple, w_pg, w_pe, g_final)


def _pad_lanes(a, width):
    return jnp.pad(a, ((0, 0), (0, width - a.shape[1])))


def kernel(x, p, norm_mix, w_in, b_gate, ssd_conv_w, ssd_conv_b, dt_bias, a_log, d_skip, ssd_norm, w_ssd_out, cm_conv_w, cm_conv_b, cm_ln_g, cm_ln_b, w_cm_out, w_out, norm_ffn, w_up, ffn_conv_w, ffn_conv_b, w_down, norm_ple, w_ple_gate, w_ple_proj, final_norm):
    bsz, seq, d = x.shape
    depth = w_in.shape[0]
    ntok = bsz * seq
    dt0 = SSD_INNER + SSD_CONV_DIM
    x2d = x.reshape(ntok, d)
    for i in range(depth):
        w_main = jnp.concatenate([w_in[i, :, :dt0], w_in[i, :, dt0 + SSD_HEADS:]], axis=1).astype(BF16)
        w_dt = _pad_lanes(w_in[i, :, dt0:dt0 + SSD_HEADS], LANES).astype(BF16)
        z, xbc, v, gs, dt = _inproj(
            x2d, norm_mix[i][None], w_main, w_dt, b_gate[i].reshape(1, -1),
            _pad_lanes(dt_bias[i][None], LANES))
        ya = _ssd(z, xbc, dt, ssd_conv_w[i], ssd_conv_b[i][None],
                  _pad_lanes(a_log[i][None], LANES),
                  jnp.repeat(d_skip[i], SSD_HEAD_DIM)[None], ssd_norm[i][None], bsz, seq)
        x2d = _cm_merge(
            v, ya, gs, x2d, jnp.pad(cm_conv_w[i], ((0, 1), (0, 0))), cm_conv_b[i][None],
            cm_ln_g[i][None], cm_ln_b[i][None], w_ssd_out[i].astype(BF16),
            w_cm_out[i].astype(BF16), w_out[i].astype(BF16), bsz, seq)
        x2d = _ffn(
            x2d, p[i].reshape(ntok, PLE_DIM), norm_ffn[i][None], w_up[i].astype(BF16),
            ffn_conv_w[i], ffn_conv_b[i][None], w_down[i].astype(BF16), norm_ple[i][None],
            w_ple_gate[i].astype(BF16), w_ple_proj[i].astype(BF16), final_norm[None],
            bsz, seq, final=(i == depth - 1))
    return x2d.reshape(bsz, seq, d)
```

```python
import functools

import jax
import jax.numpy as jnp
from jax import lax
from jax.experimental import pallas as pl
from jax.experimental.pallas import tpu as pltpu

F32 = jnp.float32
BF16 = jnp.bfloat16

D_MODEL = 1024
PLE_DIM = 256
SSD_HEADS = 16
SSD_HEAD_DIM = 64
SSD_INNER = SSD_HEADS * SSD_HEAD_DIM
SSD_GROUPS = 4
SSD_STATE = 128
SSD_CONV = 4
SSD_CHUNK = 128
SSD_CONV_DIM = SSD_INNER + 2 * SSD_GROUPS * SSD_STATE
GROUP_WIDTH = SSD_INNER // SSD_GROUPS
CM_DIM = D_MODEL
CM_KERNEL = 31
FFN_DIM = 2816
FFN_CONV = 3
EPS = 1e-6

LANES = 128
SUBLANES = 8
VMEM_LIMIT = 56 * 1024 * 1024

TM_IN = 512
NC_IN = 512
TS_SSD = 512
TM_MERGE = 512
RB_CM = 64
CM_HALO = 32
TS_FFN = 256
CW_FFN = 256


def _sigmoid(x):
    return jax.nn.sigmoid(x)


def _rms(xf, g):
    ms = jnp.mean(xf * xf, axis=-1, keepdims=True)
    return xf * lax.rsqrt(ms + EPS) * g


def _dot(a, b):
    return jnp.dot(a, b, preferred_element_type=F32)


def _split_bf16(x):
    hi = x.astype(BF16)
    lo = (x - hi.astype(F32)).astype(BF16)
    return hi, lo


def _shift_rows(ext, shift):
    return pltpu.roll(ext, shift, 0)[SUBLANES:, :]


def _const_spec(shape):
    nd = len(shape)
    return pl.BlockSpec(shape, lambda *_: (0,) * nd, pipeline_mode=pl.Buffered(1))


def _conv31_block(vext_ref, cw_ref, cb_ref, conv_ref, rb, lb):
    win_rows = RB_CM + CM_HALO
    first = CM_HALO - (CM_KERNEL - 1)
    lanes = slice(lb * LANES, (lb + 1) * LANES)
    win = vext_ref[rb * RB_CM:rb * RB_CM + win_rows, lanes]
    acc = jnp.broadcast_to(cb_ref[:, lanes], (RB_CM, LANES))
    for r in range(SUBLANES):
        wr = win if r == 0 else pltpu.roll(win, win_rows - r, 0)
        for q in range(win_rows // SUBLANES):
            k = SUBLANES * q + r - first
            if 0 <= k < CM_KERNEL:
                acc = acc + cw_ref[k:k + 1, lanes] * wr[SUBLANES * q:SUBLANES * q + RB_CM, :]
    conv_ref[rb * RB_CM:(rb + 1) * RB_CM, lanes] = acc


def _inproj_kernel(x_ref, g_ref, w_ref, wdt_ref, bg_ref, dtb_ref, cw_ref, cb_ref, lng_ref, lnb_ref,
                   z_ref, xbc_ref, yb_ref, gs_ref, dt_ref, h_ref, vext_ref, conv_ref,
                   *, tiles_per_seq):
    @pl.when(pl.program_id(0) % tiles_per_seq == 0)
    def _():
        vext_ref[0:CM_HALO, :] = jnp.zeros((CM_HALO, CM_DIM), F32)

    h_ref[...] = _rms(x_ref[...], g_ref[...]).astype(BF16)

    def mm(c0):
        return _dot(h_ref[...], w_ref[:, c0:c0 + NC_IN])

    z0, xbc0 = 0, SSD_INNER
    val0 = xbc0 + SSD_CONV_DIM
    gate0 = val0 + CM_DIM
    gs0 = gate0 + CM_DIM
    for j in range(0, CM_DIM, NC_IN):
        vext_ref[CM_HALO:CM_HALO + TM_IN, j:j + NC_IN] = mm(val0 + j) * _sigmoid(mm(gate0 + j))

    def z_chunk(j):
        z_ref[:, j:j + NC_IN] = mm(z0 + j).astype(BF16)

    def xbc_chunk(j):
        xbc_ref[:, j:j + NC_IN] = mm(xbc0 + j).astype(BF16)

    def gs_chunk(j):
        gs_ref[:, j:j + NC_IN] = _sigmoid(mm(gs0 + j) + bg_ref[:, j:j + NC_IN]).astype(BF16)

    chunks = ([functools.partial(z_chunk, j) for j in range(0, SSD_INNER, NC_IN)]
              + [functools.partial(xbc_chunk, j) for j in range(0, SSD_CONV_DIM, NC_IN)]
              + [functools.partial(gs_chunk, j) for j in range(0, 2 * D_MODEL, NC_IN)])
    blocks = [(rb, lb) for lb in range(CM_DIM // LANES) for rb in range(TM_IN // RB_CM)]
    per_chunk = -(-len(blocks) // len(chunks))
    for i, chunk in enumerate(chunks):
        chunk()
        for rb, lb in blocks[i * per_chunk:(i + 1) * per_chunk]:
            _conv31_block(vext_ref, cw_ref, cb_ref, conv_ref, rb, lb)
    vext_ref[0:CM_HALO, :] = vext_ref[TM_IN:TM_IN + CM_HALO, :]

    dtr = _dot(h_ref[...], wdt_ref[...]) + dtb_ref[...]
    dt = jnp.maximum(dtr, 0.0) + jnp.log1p(jnp.exp(-jnp.abs(dtr)))
    lane = lax.broadcasted_iota(jnp.int32, dt.shape, 1)
    dt_ref[...] = jnp.where(lane < SSD_HEADS, dt, 0.0)

    c = conv_ref[...]
    mu = jnp.mean(c, axis=-1, keepdims=True)
    d = c - mu
    var = jnp.mean(d * d, axis=-1, keepdims=True)
    yb = d * lax.rsqrt(var + EPS) * lng_ref[...] + lnb_ref[...]
    yb_ref[...] = (yb * _sigmoid(yb)).astype(BF16)


def _inproj(x2d, g, w_main, w_dt, b_gate, dt_bias, conv_w, conv_b, ln_g, ln_b, seq):
    ntok = x2d.shape[0]
    ncols = w_main.shape[1]
    row = lambda i: (i, 0)
    return pl.pallas_call(
        functools.partial(_inproj_kernel, tiles_per_seq=seq // TM_IN),
        grid=(ntok // TM_IN,),
        in_specs=[
            pl.BlockSpec((TM_IN, D_MODEL), row),
            _const_spec((1, D_MODEL)),
            _const_spec((D_MODEL, ncols)),
            _const_spec((D_MODEL, LANES)),
            _const_spec((1, 2 * D_MODEL)),
            _const_spec((1, LANES)),
            _const_spec((CM_KERNEL + 1, CM_DIM)),
            _const_spec((1, CM_DIM)),
            _const_spec((1, CM_DIM)),
            _const_spec((1, CM_DIM)),
        ],
        out_specs=[
            pl.BlockSpec((TM_IN, SSD_INNER), row),
            pl.BlockSpec((TM_IN, SSD_CONV_DIM), row),
            pl.BlockSpec((TM_IN, CM_DIM), row),
            pl.BlockSpec((TM_IN, 2 * D_MODEL), row),
            pl.BlockSpec((TM_IN, LANES), row),
        ],
        out_shape=[
            jax.ShapeDtypeStruct((ntok, SSD_INNER), BF16),
            jax.ShapeDtypeStruct((ntok, SSD_CONV_DIM), BF16),
            jax.ShapeDtypeStruct((ntok, CM_DIM), BF16),
            jax.ShapeDtypeStruct((ntok, 2 * D_MODEL), BF16),
            jax.ShapeDtypeStruct((ntok, LANES), F32),
        ],
        scratch_shapes=[
            pltpu.VMEM((TM_IN, D_MODEL), BF16),
            pltpu.VMEM((CM_HALO + TM_IN, CM_DIM), F32),
            pltpu.VMEM((TM_IN, CM_DIM), F32),
        ],
        compiler_params=pltpu.CompilerParams(
            dimension_semantics=("arbitrary",), vmem_limit_bytes=VMEM_LIMIT),
        name="inproj",
    )(x2d, g, w_main, w_dt, b_gate, dt_bias, conv_w, conv_b, ln_g, ln_b)


def _ssd_kernel(z_ref, xbc_ref, dt_ref, cw_ref, cb_ref, alog_ref, dskip_ref, nw_ref,
                out_ref, state_ref, tail_ref, xcv_ref, xdt_ref, xw_ref, eac_ref):
    L = SSD_CHUNK

    @pl.when(pl.program_id(1) == 0)
    def _():
        state_ref[...] = jnp.zeros_like(state_ref)
        tail_ref[...] = jnp.zeros_like(tail_ref)

    li = lax.broadcasted_iota(jnp.int32, (L, L), 0)
    si = lax.broadcasted_iota(jnp.int32, (L, L), 1)
    causal = si <= li
    tri = causal.astype(BF16)
    hrow = lax.broadcasted_iota(jnp.int32, (LANES, SSD_INNER), 0)
    hcol = lax.broadcasted_iota(jnp.int32, (LANES, SSD_INNER), 1)
    head_of = jnp.right_shift(hcol, SSD_HEAD_DIM.bit_length() - 1)
    expand = (head_of == hrow).astype(BF16)
    low_half = si < SSD_HEAD_DIM
    a_row = -jnp.exp(alog_ref[...])

    def chunk(ci, carry):
        rows = pl.ds(pl.multiple_of(ci * L, L), L)

        cblk = 512
        for c0 in range(0, SSD_CONV_DIM, cblk):
            cols = slice(c0, c0 + cblk)
            u = xbc_ref[rows, cols].astype(F32)
            ext = jnp.concatenate([tail_ref[:, cols], u], axis=0)
            acc = cb_ref[:, cols] + cw_ref[SSD_CONV - 1:SSD_CONV, cols] * u
            for sft in range(1, SSD_CONV):
                k = SSD_CONV - 1 - sft
                acc = acc + cw_ref[k:k + 1, cols] * _shift_rows(ext, sft)
            tail_ref[:, cols] = u[L - SUBLANES:, :]
            xcv_ref[:, cols] = acc * _sigmoid(acc)

        dtc = dt_ref[rows, :]
        da_hi, da_lo = _split_bf16(dtc * a_row)
        acum = _dot(tri, da_hi) + _dot(tri, da_lo)
        acum_t = acum.T
        both_hi, both_lo = _split_bf16(jnp.concatenate([dtc, acum], axis=0))
        ex = _dot(both_hi, expand) + _dot(both_lo, expand)
        dt_exp = ex[:L]
        ac_exp = ex[L:]
        a_last = ac_exp[L - 1:L, :]
        xdt = xcv_ref[:, :SSD_INNER] * dt_exp
        xdt_ref[...] = xdt.astype(BF16)
        xw_ref[...] = (xdt * jnp.exp(a_last - ac_exp)).astype(BF16)
        eac_ref[...] = jnp.exp(ac_exp)
        chunk_decay = jnp.exp(a_last)

        for g in range(SSD_GROUPS):
            gcols = slice(g * GROUP_WIDTH, (g + 1) * GROUP_WIDTH)
            b0 = SSD_INNER + g * SSD_STATE
            c0 = SSD_INNER + SSD_GROUPS * SSD_STATE + g * SSD_STATE
            bm_f = xcv_ref[:, b0:b0 + SSD_STATE]
            bm_b = bm_f.astype(BF16)
            bmt_b = bm_f.T.astype(BF16)
            cm_b = xcv_ref[:, c0:c0 + SSD_STATE].astype(BF16)
            cbm = lax.dot_general(cm_b, bm_b, (((1,), (1,)), ((), ())),
                                  preferred_element_type=F32)
            st = state_ref[g]
            y_off = _dot(cm_b, st.astype(BF16)) * eac_ref[:, gcols]
            state_ref[g] = st * chunk_decay[:, gcols] + _dot(bmt_b, xw_ref[:, gcols])

            y_pairs = []
            for jj in range(2):
                j = 2 * g + jj
                scores = []
                for h in (2 * j, 2 * j + 1):
                    seg = acum[:, h:h + 1] - acum_t[h:h + 1, :]
                    decay = jnp.exp(jnp.where(causal, seg, -jnp.inf))
                    scores.append((cbm * decay).astype(BF16))
                lhs = jnp.concatenate(scores, axis=1)
                xp = xdt_ref[:, j * LANES:(j + 1) * LANES]
                zero = jnp.zeros_like(xp)
                rhs = jnp.concatenate([jnp.where(low_half, xp, zero),
                                       jnp.where(low_half, zero, xp)], axis=0)
                y_pairs.append(_dot(lhs, rhs))
            y = (jnp.concatenate(y_pairs, axis=1) + y_off
                 + xcv_ref[:, gcols] * dskip_ref[:, gcols])
            zf = z_ref[rows, gcols].astype(F32)
            yz = y * (zf * _sigmoid(zf))
            ms = jnp.mean(yz * yz, axis=-1, keepdims=True)
            out_ref[rows, gcols] = (yz * lax.rsqrt(ms + EPS) * nw_ref[:, gcols]).astype(BF16)
        return carry

    lax.fori_loop(0, TS_SSD // L, chunk, 0)


def _ssd(z, xbc, dt, conv_w, conv_b, a_log, d_skip, norm_w, bsz, seq):
    ntok = z.shape[0]
    nt = seq // TS_SSD
    row = lambda b, s: (b * nt + s, 0)
    return pl.pallas_call(
        _ssd_kernel,
        grid=(bsz, nt),
        in_specs=[
            pl.BlockSpec((TS_SSD, SSD_INNER), row),
            pl.BlockSpec((TS_SSD, SSD_CONV_DIM), row),
            pl.BlockSpec((TS_SSD, LANES), row),
            _const_spec((SSD_CONV, SSD_CONV_DIM)),
            _const_spec((1, SSD_CONV_DIM)),
            _const_spec((1, LANES)),
            _const_spec((1, SSD_INNER)),
            _const_spec((1, SSD_INNER)),
        ],
        out_specs=pl.BlockSpec((TS_SSD, SSD_INNER), row),
        out_shape=jax.ShapeDtypeStruct((ntok, SSD_INNER), BF16),
        scratch_shapes=[
            pltpu.VMEM((SSD_GROUPS, SSD_STATE, GROUP_WIDTH), F32),
            pltpu.VMEM((SUBLANES, SSD_CONV_DIM), F32),
            pltpu.VMEM((SSD_CHUNK, SSD_CONV_DIM), F32),
            pltpu.VMEM((SSD_CHUNK, SSD_INNER), BF16),
            pltpu.VMEM((SSD_CHUNK, SSD_INNER), BF16),
            pltpu.VMEM((SSD_CHUNK, SSD_INNER), F32),
        ],
        compiler_params=pltpu.CompilerParams(
            dimension_semantics=("arbitrary", "arbitrary"), vmem_limit_bytes=VMEM_LIMIT),
        name="ssd",
    )(z, xbc, dt, conv_w, conv_b, a_log, d_skip, norm_w)


def _merge_kernel(yb_ref, ya_ref, gs_ref, x_ref, wa_ref, wb_ref, wo_ref, out_ref):
    pa = _dot(ya_ref[...], wa_ref[...])
    pb = _dot(yb_ref[...], wb_ref[...])
    merged = (gs_ref[:, :D_MODEL].astype(F32) * pa + gs_ref[:, D_MODEL:].astype(F32) * pb)
    out_ref[...] = x_ref[...] + _dot(merged.astype(BF16), wo_ref[...])


def _merge(yb, ya, gs, x2d, w_a, w_b, w_o):
    ntok = yb.shape[0]
    row = lambda i: (i, 0)
    return pl.pallas_call(
        _merge_kernel,
        grid=(ntok // TM_MERGE,),
        in_specs=[
            pl.BlockSpec((TM_MERGE, CM_DIM), row),
            pl.BlockSpec((TM_MERGE, SSD_INNER), row),
            pl.BlockSpec((TM_MERGE, 2 * D_MODEL), row),
            pl.BlockSpec((TM_MERGE, D_MODEL), row),
            _const_spec((SSD_INNER, D_MODEL)),
            _const_spec((CM_DIM, D_MODEL)),
            _const_spec((D_MODEL, D_MODEL)),
        ],
        out_specs=pl.BlockSpec((TM_MERGE, D_MODEL), row),
        out_shape=jax.ShapeDtypeStruct((ntok, D_MODEL), F32),
        compiler_params=pltpu.CompilerParams(
            dimension_semantics=("arbitrary",), vmem_limit_bytes=VMEM_LIMIT),
        name="merge",
    )(yb, ya, gs, x2d, w_a, w_b, w_o)


def _ffn_kernel(x_ref, p_ref, gf_ref, wup_ref, fcw_ref, fcb_ref, wdn_ref, gp_ref, wpg_ref,
                wpe_ref, gfin_ref, out_ref, h_ref, tail_ref, acc_ref, *, final):
    @pl.when(pl.program_id(1) == 0)
    def _():
        tail_ref[...] = jnp.zeros_like(tail_ref)

    xf = x_ref[...]
    h_ref[...] = _rms(xf, gf_ref[...]).astype(BF16)
    acc_ref[...] = xf

    def up_cols(c, half):
        return slice(half * FFN_DIM + c * CW_FFN, half * FFN_DIM + (c + 1) * CW_FFN)

    def up(c):
        return [_dot(h_ref[...], wup_ref[:, up_cols(c, half)]) for half in range(2)]

    n_chunks = FFN_DIM // CW_FFN
    pending = up(0)
    for c in range(n_chunks):
        current = pending
        if c + 1 < n_chunks:
            pending = up(c + 1)
        halves = []
        for half, u in enumerate(current):
            cols = up_cols(c, half)
            ext = jnp.concatenate([tail_ref[:, cols], u], axis=0)
            y = fcb_ref[:, cols] + fcw_ref[FFN_CONV - 1:FFN_CONV, cols] * u
            for sft in range(1, FFN_CONV):
                k = FFN_CONV - 1 - sft
                y = y + fcw_ref[k:k + 1, cols] * _shift_rows(ext, sft)
            tail_ref[:, cols] = u[TS_FFN - SUBLANES:, :]
            halves.append(y)
        gate, val = halves
        act = (gate * _sigmoid(gate) * val).astype(BF16)
        acc_ref[...] += _dot(act, wdn_ref[c * CW_FFN:(c + 1) * CW_FFN, :])
    x1 = acc_ref[...]
    h2 = _rms(x1, gp_ref[...]).astype(BF16)
    pg = _sigmoid(_dot(h2, wpg_ref[...]))
    pe = _dot(p_ref[...].astype(BF16), wpe_ref[...])
    x2 = x1 + pg * pe
    if final:
        x2 = _rms(x2, gfin_ref[...])
    out_ref[...] = x2


def _ffn(x2d, p2d, g_ffn, w_up, conv_w, conv_b, w_down, g_ple, w_pg, w_pe, g_final,
         bsz, seq, final):
    ntok = x2d.shape[0]
    nt = seq // TS_FFN
    row = lambda b, s: (b * nt + s, 0)
    return pl.pallas_call(
        functools.partial(_ffn_kernel, final=final),
        grid=(bsz, nt),
        in_specs=[
            pl.BlockSpec((TS_FFN, D_MODEL), row),
            pl.BlockSpec((TS_FFN, PLE_DIM), row),
            _const_spec((1, D_MODEL)),
            _const_spec((D_MODEL, 2 * FFN_DIM)),
            _const_spec((FFN_CONV, 2 * FFN_DIM)),
            _const_spec((1, 2 * FFN_DIM)),
            _const_spec((FFN_DIM, D_MODEL)),
            _const_spec((1, D_MODEL)),
            _const_spec((D_MODEL, D_MODEL)),
            _const_spec((PLE_DIM, D_MODEL)),
            _const_spec((1, D_MODEL)),
        ],
        out_specs=pl.BlockSpec((TS_FFN, D_MODEL), row),
        out_shape=jax.ShapeDtypeStruct((ntok, D_MODEL), F32),
        scratch_shapes=[
            pltpu.VMEM((TS_FFN, D_MODEL), BF16),
            pltpu.VMEM((SUBLANES, 2 * FFN_DIM), F32),
            pltpu.VMEM((TS_FFN, D_MODEL), F32),
        ],
        compiler_params=pltpu.CompilerParams(
            dimension_semantics=("arbitrary", "arbitrary"), vmem_limit_bytes=VMEM_LIMIT),
        name="ffn_ple",
    )(x2d, p2d, g_ffn, w_up, conv_w, conv_b, w_down, g_ple, w_pg, w_pe, g_final)


def _pad_lanes(a, width):
    return jnp.pad(a, ((0, 0), (0, width - a.shape[1])))


def kernel(x, p, norm_mix, w_in, b_gate, ssd_conv_w, ssd_conv_b, dt_bias, a_log, d_skip, ssd_norm, w_ssd_out, cm_conv_w, cm_conv_b, cm_ln_g, cm_ln_b, w_cm_out, w_out, norm_ffn, w_up, ffn_conv_w, ffn_conv_b, w_down, norm_ple, w_ple_gate, w_ple_proj, final_norm):
    bsz, seq, d = x.shape
    depth = w_in.shape[0]
    ntok = bsz * seq
    dt0 = SSD_INNER + SSD_CONV_DIM
    x2d = x.reshape(ntok, d)
    for i in range(depth):
        w_main = jnp.concatenate([w_in[i, :, :dt0], w_in[i, :, dt0 + SSD_HEADS:]], axis=1).astype(BF16)
        w_dt = _pad_lanes(w_in[i, :, dt0:dt0 + SSD_HEADS], LANES).astype(BF16)
        z, xbc, yb, gs, dt = _inproj(
            x2d, norm_mix[i][None], w_main, w_dt, b_gate[i].reshape(1, -1),
            _pad_lanes(dt_bias[i][None], LANES), jnp.pad(cm_conv_w[i], ((0, 1), (0, 0))),
            cm_conv_b[i][None], cm_ln_g[i][None], cm_ln_b[i][None], seq)
        ya = _ssd(z, xbc, dt, ssd_conv_w[i], ssd_conv_b[i][None],
                  _pad_lanes(a_log[i][None], LANES),
                  jnp.repeat(d_skip[i], SSD_HEAD_DIM)[None], ssd_norm[i][None], bsz, seq)
        x2d = _merge(yb, ya, gs, x2d, w_ssd_out[i].astype(BF16), w_cm_out[i].astype(BF16),
                     w_out[i].astype(BF16))
        x2d = _ffn(
            x2d, p[i].reshape(ntok, PLE_DIM), norm_ffn[i][None], w_up[i].astype(BF16),
            ffn_conv_w[i], ffn_conv_b[i][None], w_down[i].astype(BF16), norm_ple[i][None],
            w_ple_gate[i].astype(BF16), w_ple_proj[i].astype(BF16), final_norm[None],
            bsz, seq, final=(i == depth - 1))
    return x2d.reshape(bsz, seq, d)
```

```python
import functools

import jax
import jax.numpy as jnp
from jax import lax
from jax.experimental import pallas as pl
from jax.experimental.pallas import tpu as pltpu

F32 = jnp.float32
BF16 = jnp.bfloat16

D_MODEL = 1024
PLE_DIM = 256
SSD_HEADS = 16
SSD_HEAD_DIM = 64
SSD_INNER = SSD_HEADS * SSD_HEAD_DIM
SSD_GROUPS = 4
SSD_STATE = 128
SSD_CONV = 4
SSD_CHUNK = 128
SSD_CONV_DIM = SSD_INNER + 2 * SSD_GROUPS * SSD_STATE
GROUP_WIDTH = SSD_INNER // SSD_GROUPS
CM_DIM = D_MODEL
CM_KERNEL = 31
FFN_DIM = 2816
FFN_CONV = 3
EPS = 1e-6

LANES = 128
SUBLANES = 8
BF16_ROWS = 16
VMEM_LIMIT = 56 * 1024 * 1024

TM_IN = 512
NC_IN = 512
FUSED_DIM = SSD_CONV_DIM + 2 * D_MODEL + SSD_INNER
XBC_BLOCK = 0
GATES_BLOCK = SSD_CONV_DIM // (2 * D_MODEL)
Z_BLOCK = (SSD_CONV_DIM + 2 * D_MODEL) // SSD_INNER
TS_SSD = 512
TM_MERGE = 512
CONV_BLOCK = 128
CONV_CHANNELS = 32
CONV_UNROLL = 4
TS_FFN = 512
CW_FFN = 256
FFN_LOOKAHEAD = 2


def _sigmoid(x):
    return jax.nn.sigmoid(x)


def _rms(xf, g):
    ms = jnp.mean(xf * xf, axis=-1, keepdims=True)
    return xf * lax.rsqrt(ms + EPS) * g


def _dot(a, b):
    return jnp.dot(a, b, preferred_element_type=F32)


def _split_bf16(x):
    hi = x.astype(BF16)
    lo = (x - hi.astype(F32)).astype(BF16)
    return hi, lo


def _shift_rows(ext, shift):
    return pltpu.roll(ext, shift, 0)[SUBLANES:, :]


def _const_spec(shape):
    nd = len(shape)
    return pl.BlockSpec(shape, lambda *_: (0,) * nd, pipeline_mode=pl.Buffered(1))


def _inproj_kernel(x_ref, g_ref, w_ref, wdt_ref, dtb_ref, fused_ref, vt_ref, dt_ref, h_ref):
    h_ref[...] = _rms(x_ref[...], g_ref[...]).astype(BF16)

    def mm(c0):
        return _dot(h_ref[...], w_ref[:, c0:c0 + NC_IN])

    for j in range(0, CM_DIM, NC_IN):
        v = mm(j) * _sigmoid(mm(CM_DIM + j))
        vt_ref[j:j + NC_IN, :] = v.T.astype(BF16)
    for j in range(0, FUSED_DIM, NC_IN):
        fused_ref[:, j:j + NC_IN] = mm(2 * CM_DIM + j).astype(BF16)
    dtr = _dot(h_ref[...], wdt_ref[...]) + dtb_ref[...]
    dt = jnp.maximum(dtr, 0.0) + jnp.log1p(jnp.exp(-jnp.abs(dtr)))
    lane = lax.broadcasted_iota(jnp.int32, dt.shape, 1)
    dt_ref[...] = jnp.where(lane < SSD_HEADS, dt, 0.0)


def _inproj(x2d, g, w_all, w_dt, dt_bias):
    ntok = x2d.shape[0]
    row = lambda i: (i, 0)
    return pl.pallas_call(
        _inproj_kernel,
        grid=(ntok // TM_IN,),
        in_specs=[
            pl.BlockSpec((TM_IN, D_MODEL), row),
            _const_spec((1, D_MODEL)),
            _const_spec((D_MODEL, 2 * CM_DIM + FUSED_DIM)),
            _const_spec((D_MODEL, LANES)),
            _const_spec((1, LANES)),
        ],
        out_specs=[
            pl.BlockSpec((TM_IN, FUSED_DIM), row),
            pl.BlockSpec((CM_DIM, TM_IN), lambda i: (0, i)),
            pl.BlockSpec((TM_IN, LANES), row),
        ],
        out_shape=[
            jax.ShapeDtypeStruct((ntok, FUSED_DIM), BF16),
            jax.ShapeDtypeStruct((CM_DIM, ntok), BF16),
            jax.ShapeDtypeStruct((ntok, LANES), F32),
        ],
        scratch_shapes=[pltpu.VMEM((TM_IN, D_MODEL), BF16)],
        compiler_params=pltpu.CompilerParams(
            dimension_semantics=("arbitrary",), vmem_limit_bytes=VMEM_LIMIT),
        name="inproj",
    )(x2d, g, w_all, w_dt, dt_bias)


def _toeplitz_kernel(wt_ref, out_ref):
    ncol = out_ref.shape[1]
    q = pl.program_id(0) * ncol + lax.broadcasted_iota(jnp.int32, (LANES, ncol), 1)
    k = lax.broadcasted_iota(jnp.int32, (LANES, ncol), 0)
    s = jnp.right_shift(q, 8)
    n = jnp.bitwise_and(q, 2 * CONV_BLOCK - 1)
    lag = jnp.bitwise_and(n, CONV_BLOCK - 1) - s + jnp.bitwise_and(n, CONV_BLOCK)
    onehot = (k == (CM_KERNEL - 1) - lag).astype(BF16)
    out_ref[...] = _dot(wt_ref[...], onehot).astype(BF16)


def _toeplitz(wt):
    ncol = 2048
    total = CONV_BLOCK * 2 * CONV_BLOCK
    return pl.pallas_call(
        _toeplitz_kernel,
        grid=(total // ncol,),
        in_specs=[_const_spec((CM_DIM, LANES))],
        out_specs=pl.BlockSpec((CM_DIM, ncol), lambda i: (0, i)),
        out_shape=jax.ShapeDtypeStruct((CM_DIM, total), BF16),
        compiler_params=pltpu.CompilerParams(
            dimension_semantics=("arbitrary",), vmem_limit_bytes=VMEM_LIMIT),
        name="toeplitz",
    )(wt)


def _conv31_kernel(vt_ref, tw_ref, yt_ref, *, blocks_per_seq):
    nb = vt_ref.shape[1]
    blk = lax.broadcasted_iota(jnp.int32, (nb, CONV_BLOCK), 0)
    has_prev = jnp.bitwise_and(blk, blocks_per_seq - 1) != 0

    def channels(i, carry):
        for u in range(CONV_UNROLL):
            c = i * CONV_UNROLL + u
            z = _dot(vt_ref[c], tw_ref[c])
            from_prev = pltpu.roll(z[:, CONV_BLOCK:], 1, 0)
            yt_ref[c] = z[:, :CONV_BLOCK] + jnp.where(has_prev, from_prev, 0.0)
        return carry

    lax.fori_loop(0, vt_ref.shape[0] // CONV_UNROLL, channels, 0)


def _conv31(vt, tw, seq):
    nch, ntok = vt.shape
    nb = ntok // CONV_BLOCK
    blocks_per_seq = seq // CONV_BLOCK
    assert blocks_per_seq & (blocks_per_seq - 1) == 0
    spec = lambda width: pl.BlockSpec((CONV_CHANNELS, width[0], width[1]), lambda i: (i, 0, 0))
    yt = pl.pallas_call(
        functools.partial(_conv31_kernel, blocks_per_seq=blocks_per_seq),
        grid=(nch // CONV_CHANNELS,),
        in_specs=[spec((nb, CONV_BLOCK)), spec((CONV_BLOCK, 2 * CONV_BLOCK))],
        out_specs=spec((nb, CONV_BLOCK)),
        out_shape=jax.ShapeDtypeStruct((nch, nb, CONV_BLOCK), F32),
        compiler_params=pltpu.CompilerParams(
            dimension_semantics=("arbitrary",), vmem_limit_bytes=VMEM_LIMIT),
        name="conv31",
    )(vt.reshape(nch, nb, CONV_BLOCK), tw.reshape(nch, CONV_BLOCK, 2 * CONV_BLOCK))
    return yt.reshape(nch, ntok)


def _ssd_kernel(z_ref, xbc_ref, dt_ref, cw_ref, cb_ref, alog_ref, dskip_ref, nw_ref,
                out_ref, state_ref, tail_ref, cwb_ref, xcv_ref, xdt_ref, xw_ref, eac_ref):
    L = SSD_CHUNK

    @pl.when(pl.program_id(1) == 0)
    def _():
        state_ref[...] = jnp.zeros_like(state_ref)
        tail_ref[...] = jnp.zeros_like(tail_ref)

    for k in range(SSD_CONV):
        cwb_ref[k * BF16_ROWS:(k + 1) * BF16_ROWS, :] = jnp.broadcast_to(
            cw_ref[k:k + 1, :], (BF16_ROWS, SSD_CONV_DIM)).astype(BF16)

    li = lax.broadcasted_iota(jnp.int32, (L, L), 0)
    si = lax.broadcasted_iota(jnp.int32, (L, L), 1)
    causal = si <= li
    tri = causal.astype(BF16)
    hrow = lax.broadcasted_iota(jnp.int32, (LANES, SSD_INNER), 0)
    hcol = lax.broadcasted_iota(jnp.int32, (LANES, SSD_INNER), 1)
    head_of = jnp.right_shift(hcol, SSD_HEAD_DIM.bit_length() - 1)
    expand = (head_of == hrow).astype(BF16)
    low_half = si < SSD_HEAD_DIM
    ext_rows = BF16_ROWS + L
    shift_k = -(-SSD_CONV * ext_rows // LANES) * LANES
    st = lax.broadcasted_iota(jnp.int32, (L, shift_k), 0)
    sc = lax.broadcasted_iota(jnp.int32, (L, shift_k), 1)
    hit = sc < 0
    for k in range(SSD_CONV):
        hit = hit | (sc == st + (k * ext_rows + BF16_ROWS - (SSD_CONV - 1 - k)))
    shift_mat = hit.astype(BF16)
    a_row = -jnp.exp(alog_ref[...])

    def chunk(ci, carry):
        rows = pl.ds(pl.multiple_of(ci * L, L), L)

        cblk = 512
        for c0 in range(0, SSD_CONV_DIM, cblk):
            cols = slice(c0, c0 + cblk)
            ext = jnp.concatenate([tail_ref[:, cols], xbc_ref[rows, cols]], axis=0)
            ext = ext.reshape(ext_rows // BF16_ROWS, BF16_ROWS, cblk)
            prods = [(ext * cwb_ref[k * BF16_ROWS:(k + 1) * BF16_ROWS, cols][None]
                      ).reshape(ext_rows, cblk) for k in range(SSD_CONV)]
            prods.append(jnp.zeros((shift_k - SSD_CONV * ext_rows, cblk), BF16))
            acc = _dot(shift_mat, jnp.concatenate(prods, axis=0)) + cb_ref[:, cols]
            tail_ref[:, cols] = xbc_ref[pl.ds(pl.multiple_of(ci * L + L - BF16_ROWS, BF16_ROWS),
                                              BF16_ROWS), cols]
            xcv_ref[:, cols] = acc * _sigmoid(acc)

        dtc = dt_ref[rows, :]
        da_hi, da_lo = _split_bf16(dtc * a_row)
        acum = _dot(tri, da_hi) + _dot(tri, da_lo)
        acum_t = acum.T
        both_hi, both_lo = _split_bf16(jnp.concatenate([dtc, acum], axis=0))
        ex = _dot(both_hi, expand) + _dot(both_lo, expand)
        dt_exp = ex[:L]
        ac_exp = ex[L:]
        a_last = ac_exp[L - 1:L, :]
        xdt = xcv_ref[:, :SSD_INNER] * dt_exp
        xdt_ref[...] = xdt.astype(BF16)
        xw_ref[...] = (xdt * jnp.exp(a_last - ac_exp)).astype(BF16)
        eac_ref[...] = jnp.exp(ac_exp)
        chunk_decay = jnp.exp(a_last)

        for g in range(SSD_GROUPS):
            gcols = slice(g * GROUP_WIDTH, (g + 1) * GROUP_WIDTH)
            b0 = SSD_INNER + g * SSD_STATE
            c0 = SSD_INNER + SSD_GROUPS * SSD_STATE + g * SSD_STATE
            bm_f = xcv_ref[:, b0:b0 + SSD_STATE]
            bm_b = bm_f.astype(BF16)
            bmt_b = bm_f.T.astype(BF16)
            cm_b = xcv_ref[:, c0:c0 + SSD_STATE].astype(BF16)
            cbm = lax.dot_general(cm_b, bm_b, (((1,), (1,)), ((), ())),
                                  preferred_element_type=F32)
            st = state_ref[g]
            y_off = _dot(cm_b, st.astype(BF16)) * eac_ref[:, gcols]
            state_ref[g] = st * chunk_decay[:, gcols] + _dot(bmt_b, xw_ref[:, gcols])

            y_pairs = []
            for jj in range(2):
                j = 2 * g + jj
                scores = []
                for h in (2 * j, 2 * j + 1):
                    seg = acum[:, h:h + 1] - acum_t[h:h + 1, :]
                    decay = jnp.exp(jnp.where(causal, seg, -jnp.inf))
                    scores.append((cbm * decay).astype(BF16))
                lhs = jnp.concatenate(scores, axis=1)
                xp = xdt_ref[:, j * LANES:(j + 1) * LANES]
                zero = jnp.zeros_like(xp)
                rhs = jnp.concatenate([jnp.where(low_half, xp, zero),
                                       jnp.where(low_half, zero, xp)], axis=0)
                y_pairs.append(_dot(lhs, rhs))
            y = (jnp.concatenate(y_pairs, axis=1) + y_off
                 + xcv_ref[:, gcols] * dskip_ref[:, gcols])
            zf = z_ref[rows, gcols].astype(F32)
            yz = y * (zf * _sigmoid(zf))
            ms = jnp.mean(yz * yz, axis=-1, keepdims=True)
            out_ref[rows, gcols] = (yz * lax.rsqrt(ms + EPS) * nw_ref[:, gcols]).astype(BF16)
        return carry

    lax.fori_loop(0, TS_SSD // L, chunk, 0)


def _ssd(fused, dt, conv_w, conv_b, a_log, d_skip, norm_w, bsz, seq):
    ntok = fused.shape[0]
    nt = seq // TS_SSD
    row = lambda b, s: (b * nt + s, 0)
    return pl.pallas_call(
        _ssd_kernel,
        grid=(bsz, nt),
        in_specs=[
            pl.BlockSpec((TS_SSD, SSD_INNER), lambda b, s: (b * nt + s, Z_BLOCK)),
            pl.BlockSpec((TS_SSD, SSD_CONV_DIM), lambda b, s: (b * nt + s, XBC_BLOCK)),
            pl.BlockSpec((TS_SSD, LANES), row),
            _const_spec((SSD_CONV, SSD_CONV_DIM)),
            _const_spec((1, SSD_CONV_DIM)),
            _const_spec((1, LANES)),
            _const_spec((1, SSD_INNER)),
            _const_spec((1, SSD_INNER)),
        ],
        out_specs=pl.BlockSpec((TS_SSD, SSD_INNER), row),
        out_shape=jax.ShapeDtypeStruct((ntok, SSD_INNER), BF16),
        scratch_shapes=[
            pltpu.VMEM((SSD_GROUPS, SSD_STATE, GROUP_WIDTH), F32),
            pltpu.VMEM((BF16_ROWS, SSD_CONV_DIM), BF16),
            pltpu.VMEM((SSD_CONV * BF16_ROWS, SSD_CONV_DIM), BF16),
            pltpu.VMEM((SSD_CHUNK, SSD_CONV_DIM), F32),
            pltpu.VMEM((SSD_CHUNK, SSD_INNER), BF16),
            pltpu.VMEM((SSD_CHUNK, SSD_INNER), BF16),
            pltpu.VMEM((SSD_CHUNK, SSD_INNER), F32),
        ],
        compiler_params=pltpu.CompilerParams(
            dimension_semantics=("arbitrary", "arbitrary"), vmem_limit_bytes=VMEM_LIMIT),
        name="ssd",
    )(fused, fused, dt, conv_w, conv_b, a_log, d_skip, norm_w)


def _merge_kernel(yt_ref, ya_ref, gs_ref, bg_ref, x_ref, cb_ref, lng_ref, lnb_ref,
                  wa_ref, wb_ref, wo_ref, out_ref):
    c = yt_ref[...].T + cb_ref[...]
    mu = jnp.mean(c, axis=-1, keepdims=True)
    d = c - mu
    var = jnp.mean(d * d, axis=-1, keepdims=True)
    yb = d * lax.rsqrt(var + EPS) * lng_ref[...] + lnb_ref[...]
    yb = (yb * _sigmoid(yb)).astype(BF16)
    pa = _dot(ya_ref[...], wa_ref[...])
    pb = _dot(yb, wb_ref[...])
    ga = _sigmoid(gs_ref[:, :D_MODEL].astype(F32) + bg_ref[:, :D_MODEL])
    gb = _sigmoid(gs_ref[:, D_MODEL:].astype(F32) + bg_ref[:, D_MODEL:])
    merged = ga * pa + gb * pb
    out_ref[...] = x_ref[...] + _dot(merged.astype(BF16), wo_ref[...])


def _merge(yt, ya, fused, b_gate, x2d, conv_b, ln_g, ln_b, w_a, w_b, w_o):
    ntok = ya.shape[0]
    row = lambda i: (i, 0)
    return pl.pallas_call(
        _merge_kernel,
        grid=(ntok // TM_MERGE,),
        in_specs=[
            pl.BlockSpec((CM_DIM, TM_MERGE), lambda i: (0, i)),
            pl.BlockSpec((TM_MERGE, SSD_INNER), row),
            pl.BlockSpec((TM_MERGE, 2 * D_MODEL), lambda i: (i, GATES_BLOCK)),
            _const_spec((1, 2 * D_MODEL)),
            pl.BlockSpec((TM_MERGE, D_MODEL), row),
            _const_spec((1, CM_DIM)),
            _const_spec((1, CM_DIM)),
            _const_spec((1, CM_DIM)),
            _const_spec((SSD_INNER, D_MODEL)),
            _const_spec((CM_DIM, D_MODEL)),
            _const_spec((D_MODEL, D_MODEL)),
        ],
        out_specs=pl.BlockSpec((TM_MERGE, D_MODEL), row),
        out_shape=jax.ShapeDtypeStruct((ntok, D_MODEL), F32),
        compiler_params=pltpu.CompilerParams(
            dimension_semantics=("arbitrary",), vmem_limit_bytes=VMEM_LIMIT),
        name="merge",
    )(yt, ya, fused, b_gate, x2d, conv_b, ln_g, ln_b, w_a, w_b, w_o)


def _ffn_kernel(x_ref, p_ref, gf_ref, wup_ref, fcw_ref, fcb_ref, wdn_ref, gp_ref, wpg_ref,
                wpe_ref, gfin_ref, out_ref, h_ref, tail_ref, acc_ref, *, final):
    @pl.when(pl.program_id(1) == 0)
    def _():
        tail_ref[...] = jnp.zeros_like(tail_ref)

    xf = x_ref[...]
    h_ref[...] = _rms(xf, gf_ref[...]).astype(BF16)
    acc_ref[...] = xf

    def up_cols(c, half):
        return slice(half * FFN_DIM + c * CW_FFN, half * FFN_DIM + (c + 1) * CW_FFN)

    def up(c):
        return [_dot(h_ref[...], wup_ref[:, up_cols(c, half)]) for half in range(2)]

    n_chunks = FFN_DIM // CW_FFN
    pending = [up(c) for c in range(FFN_LOOKAHEAD)]
    for c in range(n_chunks):
        if c + FFN_LOOKAHEAD < n_chunks:
            pending.append(up(c + FFN_LOOKAHEAD))
        current = pending.pop(0)
        halves = []
        for half, u in enumerate(current):
            cols = up_cols(c, half)
            ext = jnp.concatenate([tail_ref[:, cols], u], axis=0)
            y = fcb_ref[:, cols] + fcw_ref[FFN_CONV - 1:FFN_CONV, cols] * u
            for sft in range(1, FFN_CONV):
                k = FFN_CONV - 1 - sft
                y = y + fcw_ref[k:k + 1, cols] * _shift_rows(ext, sft)
            tail_ref[:, cols] = u[TS_FFN - SUBLANES:, :]
            halves.append(y)
        gate, val = halves
        act = (gate * _sigmoid(gate) * val).astype(BF16)
        acc_ref[...] += _dot(act, wdn_ref[c * CW_FFN:(c + 1) * CW_FFN, :])
    x1 = acc_ref[...]
    h2 = _rms(x1, gp_ref[...]).astype(BF16)
    pg = _sigmoid(_dot(h2, wpg_ref[...]))
    pe = _dot(p_ref[...].astype(BF16), wpe_ref[...])
    x2 = x1 + pg * pe
    if final:
        x2 = _rms(x2, gfin_ref[...])
    out_ref[...] = x2


def _ffn(x2d, p2d, g_ffn, w_up, conv_w, conv_b, w_down, g_ple, w_pg, w_pe, g_final,
         bsz, seq, final):
    ntok = x2d.shape[0]
    nt = seq // TS_FFN
    row = lambda b, s: (b * nt + s, 0)
    return pl.pallas_call(
        functools.partial(_ffn_kernel, final=final),
        grid=(bsz, nt),
        in_specs=[
            pl.BlockSpec((TS_FFN, D_MODEL), row),
            pl.BlockSpec((TS_FFN, PLE_DIM), row),
            _const_spec((1, D_MODEL)),
            _const_spec((D_MODEL, 2 * FFN_DIM)),
            _const_spec((FFN_CONV, 2 * FFN_DIM)),
            _const_spec((1, 2 * FFN_DIM)),
            _const_spec((FFN_DIM, D_MODEL)),
            _const_spec((1, D_MODEL)),
            _const_spec((D_MODEL, D_MODEL)),
            _const_spec((PLE_DIM, D_MODEL)),
            _const_spec((1, D_MODEL)),
        ],
        out_specs=pl.BlockSpec((TS_FFN, D_MODEL), row),
        out_shape=jax.ShapeDtypeStruct((ntok, D_MODEL), F32),
        scratch_shapes=[
            pltpu.VMEM((TS_FFN, D_MODEL), BF16),
            pltpu.VMEM((SUBLANES, 2 * FFN_DIM), F32),
            pltpu.VMEM((TS_FFN, D_MODEL), F32),
        ],
        compiler_params=pltpu.CompilerParams(
            dimension_semantics=("arbitrary", "arbitrary"), vmem_limit_bytes=VMEM_LIMIT),
        name="ffn_ple",
    )(x2d, p2d, g_ffn, w_up, conv_w, conv_b, w_down, g_ple, w_pg, w_pe, g_final)


def _pad_lanes(a, width):
    return jnp.pad(a, ((0, 0), (0, width - a.shape[1])))


def kernel(x, p, norm_mix, w_in, b_gate, ssd_conv_w, ssd_conv_b, dt_bias, a_log, d_skip, ssd_norm, w_ssd_out, cm_conv_w, cm_conv_b, cm_ln_g, cm_ln_b, w_cm_out, w_out, norm_ffn, w_up, ffn_conv_w, ffn_conv_b, w_down, norm_ple, w_ple_gate, w_ple_proj, final_norm):
    bsz, seq, d = x.shape
    depth = w_in.shape[0]
    ntok = bsz * seq
    dt0 = SSD_INNER + SSD_CONV_DIM
    x2d = x.reshape(ntok, d)
    for i in range(depth):
        wi = w_in[i]
        val0 = dt0 + SSD_HEADS
        gs0 = val0 + 2 * CM_DIM
        w_all = jnp.concatenate([wi[:, val0:gs0], wi[:, SSD_INNER:dt0], wi[:, gs0:],
                                 wi[:, :SSD_INNER]], axis=1).astype(BF16)
        w_dt = _pad_lanes(wi[:, dt0:val0], LANES).astype(BF16)
        fused, vt, dt = _inproj(x2d, norm_mix[i][None], w_all, w_dt,
                                _pad_lanes(dt_bias[i][None], LANES))
        yt = _conv31(vt, _toeplitz(_pad_lanes(cm_conv_w[i].T, LANES).astype(BF16)), seq)
        ya = _ssd(fused, dt, ssd_conv_w[i], ssd_conv_b[i][None],
                  _pad_lanes(a_log[i][None], LANES),
                  jnp.repeat(d_skip[i], SSD_HEAD_DIM)[None], ssd_norm[i][None], bsz, seq)
        x2d = _merge(yt, ya, fused, b_gate[i].reshape(1, -1), x2d, cm_conv_b[i][None],
                     cm_ln_g[i][None], cm_ln_b[i][None], w_ssd_out[i].astype(BF16),
                     w_cm_out[i].astype(BF16), w_out[i].astype(BF16))
        x2d = _ffn(
            x2d, p[i].reshape(ntok, PLE_DIM), norm_ffn[i][None], w_up[i].astype(BF16),
            ffn_conv_w[i], ffn_conv_b[i][None], w_down[i].astype(BF16), norm_ple[i][None],
            w_ple_gate[i].astype(BF16), w_ple_proj[i].astype(BF16), final_norm[None],
            bsz, seq, final=(i == depth - 1))
    return x2d.reshape(bsz, seq, d)
```

```python
import functools

import jax
import jax.numpy as jnp
from jax import lax
from jax.experimental import pallas as pl
from jax.experimental.pallas import tpu as pltpu

F32 = jnp.float32
BF16 = jnp.bfloat16

D_MODEL = 1024
PLE_DIM = 256
SSD_HEADS = 16
SSD_HEAD_DIM = 64
SSD_INNER = SSD_HEADS * SSD_HEAD_DIM
SSD_GROUPS = 4
SSD_STATE = 128
SSD_CONV = 4
SSD_CHUNK = 128
SSD_CONV_DIM = SSD_INNER + 2 * SSD_GROUPS * SSD_STATE
GROUP_WIDTH = SSD_INNER // SSD_GROUPS
CM_DIM = D_MODEL
CM_KERNEL = 31
FFN_DIM = 2816
FFN_CONV = 3
EPS = 1e-6

LANES = 128
SUBLANES = 8
BF16_ROWS = 16
VMEM_LIMIT = 56 * 1024 * 1024

TM_IN = 512
NC_IN = 512
FUSED_DIM = SSD_CONV_DIM + 2 * D_MODEL + SSD_INNER
XBC_BLOCK = 0
GATES_BLOCK = SSD_CONV_DIM // (2 * D_MODEL)
Z_BLOCK = (SSD_CONV_DIM + 2 * D_MODEL) // SSD_INNER
TS_SSD = 512
TM_MERGE = 512
CONV_BLOCK = 128
CONV_CHANNELS = 32
CONV_UNROLL = 4
TS_FFN = 256
CW_FFN = 256
CAST_ROWS = 256
FFN_LOOKAHEAD = 3


def _sigmoid(x):
    return jax.nn.sigmoid(x)


def _rms(xf, g):
    ms = jnp.mean(xf * xf, axis=-1, keepdims=True)
    return xf * lax.rsqrt(ms + EPS) * g


def _dot(a, b):
    return jnp.dot(a, b, preferred_element_type=F32)


def _split_bf16(x):
    hi = x.astype(BF16)
    lo = (x - hi.astype(F32)).astype(BF16)
    return hi, lo


def _shift_rows(ext, shift):
    return pltpu.roll(ext, shift, 0)[SUBLANES:, :]


def _const_spec(shape, layer=None):
    nd = len(shape)
    if layer is None:
        return pl.BlockSpec(shape, lambda *_: (0,) * nd, pipeline_mode=pl.Buffered(1))
    return pl.BlockSpec((None,) + shape, lambda *_: (layer,) + (0,) * nd,
                        pipeline_mode=pl.Buffered(1))


def _cast_kernel(w_ref, o_ref):
    o_ref[...] = w_ref[...].astype(BF16)


def _to_bf16(w):
    nl, r, c = w.shape
    rb = min(r, CAST_ROWS)
    spec = pl.BlockSpec((None, rb, c), lambda l, i: (l, i, 0))
    return pl.pallas_call(
        _cast_kernel,
        grid=(nl, r // rb),
        in_specs=[spec],
        out_specs=spec,
        out_shape=jax.ShapeDtypeStruct(w.shape, BF16),
        compiler_params=pltpu.CompilerParams(
            dimension_semantics=("arbitrary", "arbitrary"), vmem_limit_bytes=VMEM_LIMIT),
        name="cast_bf16",
    )(w)


def _inproj_kernel(x_ref, g_ref, w_ref, wdt_ref, dtb_ref, fused_ref, vt_ref, dt_ref, h_ref):
    h_ref[...] = _rms(x_ref[...], g_ref[...]).astype(BF16)

    def mm(c0):
        return _dot(h_ref[...], w_ref[:, c0:c0 + NC_IN])

    for j in range(0, CM_DIM, NC_IN):
        v = mm(j) * _sigmoid(mm(CM_DIM + j))
        vt_ref[j:j + NC_IN, :] = v.T.astype(BF16)
    for j in range(0, FUSED_DIM, NC_IN):
        fused_ref[:, j:j + NC_IN] = mm(2 * CM_DIM + j).astype(BF16)
    dtr = _dot(h_ref[...], wdt_ref[...]) + dtb_ref[...]
    dt = jnp.maximum(dtr, 0.0) + jnp.log1p(jnp.exp(-jnp.abs(dtr)))
    lane = lax.broadcasted_iota(jnp.int32, dt.shape, 1)
    dt_ref[...] = jnp.where(lane < SSD_HEADS, dt, 0.0)


def _inproj(x2d, g, w_all, w_dt, dt_bias, layer):
    ntok = x2d.shape[0]
    row = lambda i: (i, 0)
    return pl.pallas_call(
        _inproj_kernel,
        grid=(ntok // TM_IN,),
        in_specs=[
            pl.BlockSpec((TM_IN, D_MODEL), row),
            _const_spec((1, D_MODEL), layer),
            _const_spec((D_MODEL, 2 * CM_DIM + FUSED_DIM), layer),
            _const_spec((D_MODEL, LANES), layer),
            _const_spec((1, LANES), layer),
        ],
        out_specs=[
            pl.BlockSpec((TM_IN, FUSED_DIM), row),
            pl.BlockSpec((CM_DIM, TM_IN), lambda i: (0, i)),
            pl.BlockSpec((TM_IN, LANES), row),
        ],
        out_shape=[
            jax.ShapeDtypeStruct((ntok, FUSED_DIM), BF16),
            jax.ShapeDtypeStruct((CM_DIM, ntok), BF16),
            jax.ShapeDtypeStruct((ntok, LANES), F32),
        ],
        scratch_shapes=[pltpu.VMEM((TM_IN, D_MODEL), BF16)],
        compiler_params=pltpu.CompilerParams(
            dimension_semantics=("arbitrary",), vmem_limit_bytes=VMEM_LIMIT),
        name="inproj",
    )(x2d, g, w_all, w_dt, dt_bias)


def _toeplitz_kernel(wt_ref, out_ref):
    ncol = out_ref.shape[1]
    q = pl.program_id(1) * ncol + lax.broadcasted_iota(jnp.int32, (LANES, ncol), 1)
    k = lax.broadcasted_iota(jnp.int32, (LANES, ncol), 0)
    s = jnp.right_shift(q, 8)
    n = jnp.bitwise_and(q, 2 * CONV_BLOCK - 1)
    lag = jnp.bitwise_and(n, CONV_BLOCK - 1) - s + jnp.bitwise_and(n, CONV_BLOCK)
    onehot = (k == (CM_KERNEL - 1) - lag).astype(BF16)
    out_ref[...] = _dot(wt_ref[...], onehot).astype(BF16)


def _toeplitz(wt):
    nl = wt.shape[0]
    ncol = 2048
    total = CONV_BLOCK * 2 * CONV_BLOCK
    tw = pl.pallas_call(
        _toeplitz_kernel,
        grid=(nl, total // ncol),
        in_specs=[pl.BlockSpec((None, CM_DIM, LANES), lambda l, i: (l, 0, 0))],
        out_specs=pl.BlockSpec((None, CM_DIM, ncol), lambda l, i: (l, 0, i)),
        out_shape=jax.ShapeDtypeStruct((nl, CM_DIM, total), BF16),
        compiler_params=pltpu.CompilerParams(
            dimension_semantics=("arbitrary", "arbitrary"), vmem_limit_bytes=VMEM_LIMIT),
        name="toeplitz",
    )(wt)
    return tw.reshape(nl, CM_DIM, CONV_BLOCK, 2 * CONV_BLOCK)


def _conv31_kernel(vt_ref, tw_ref, yt_ref, *, blocks_per_seq):
    nb = vt_ref.shape[1]
    blk = lax.broadcasted_iota(jnp.int32, (nb, CONV_BLOCK), 0)
    has_prev = jnp.bitwise_and(blk, blocks_per_seq - 1) != 0

    def channels(i, carry):
        for u in range(CONV_UNROLL):
            c = i * CONV_UNROLL + u
            z = _dot(vt_ref[c], tw_ref[c])
            from_prev = pltpu.roll(z[:, CONV_BLOCK:], 1, 0)
            yt_ref[c] = (z[:, :CONV_BLOCK] + jnp.where(has_prev, from_prev, 0.0)).astype(BF16)
        return carry

    lax.fori_loop(0, vt_ref.shape[0] // CONV_UNROLL, channels, 0)


def _conv31(vt, tw, seq, layer):
    nch, ntok = vt.shape
    nb = ntok // CONV_BLOCK
    blocks_per_seq = seq // CONV_BLOCK
    assert blocks_per_seq & (blocks_per_seq - 1) == 0
    spec = pl.BlockSpec((CONV_CHANNELS, nb, CONV_BLOCK), lambda i: (i, 0, 0))
    yt = pl.pallas_call(
        functools.partial(_conv31_kernel, blocks_per_seq=blocks_per_seq),
        grid=(nch // CONV_CHANNELS,),
        in_specs=[spec, pl.BlockSpec((None, CONV_CHANNELS, CONV_BLOCK, 2 * CONV_BLOCK),
                                     lambda i: (layer, i, 0, 0))],
        out_specs=spec,
        out_shape=jax.ShapeDtypeStruct((nch, nb, CONV_BLOCK), BF16),
        compiler_params=pltpu.CompilerParams(
            dimension_semantics=("arbitrary",), vmem_limit_bytes=VMEM_LIMIT),
        name="conv31",
    )(vt.reshape(nch, nb, CONV_BLOCK), tw)
    return yt.reshape(nch, ntok)


def _ssd_kernel(z_ref, xbc_ref, dt_ref, cw_ref, cb_ref, alog_ref, dskip_ref, nw_ref,
                out_ref, state_ref, tail_ref, cwb_ref, xcv_ref, xdt_ref, xw_ref, eac_ref):
    L = SSD_CHUNK

    @pl.when(pl.program_id(1) == 0)
    def _():
        state_ref[...] = jnp.zeros_like(state_ref)
        tail_ref[...] = jnp.zeros_like(tail_ref)

    for k in range(SSD_CONV):
        cwb_ref[k * BF16_ROWS:(k + 1) * BF16_ROWS, :] = jnp.broadcast_to(
            cw_ref[k:k + 1, :], (BF16_ROWS, SSD_CONV_DIM)).astype(BF16)

    li = lax.broadcasted_iota(jnp.int32, (L, L), 0)
    si = lax.broadcasted_iota(jnp.int32, (L, L), 1)
    causal = si <= li
    tri = causal.astype(BF16)
    hrow = lax.broadcasted_iota(jnp.int32, (LANES, SSD_INNER), 0)
    hcol = lax.broadcasted_iota(jnp.int32, (LANES, SSD_INNER), 1)
    head_of = jnp.right_shift(hcol, SSD_HEAD_DIM.bit_length() - 1)
    expand = (head_of == hrow).astype(BF16)
    low_half = si < SSD_HEAD_DIM
    ext_rows = BF16_ROWS + L
    shift_k = -(-SSD_CONV * ext_rows // LANES) * LANES
    st = lax.broadcasted_iota(jnp.int32, (L, shift_k), 0)
    sc = lax.broadcasted_iota(jnp.int32, (L, shift_k), 1)
    hit = sc < 0
    for k in range(SSD_CONV):
        hit = hit | (sc == st + (k * ext_rows + BF16_ROWS - (SSD_CONV - 1 - k)))
    shift_mat = hit.astype(BF16)
    a_row = -jnp.exp(alog_ref[...])

    def chunk(ci, carry):
        rows = pl.ds(pl.multiple_of(ci * L, L), L)

        cblk = 512
        for c0 in range(0, SSD_CONV_DIM, cblk):
            cols = slice(c0, c0 + cblk)
            ext = jnp.concatenate([tail_ref[:, cols], xbc_ref[rows, cols]], axis=0)
            ext = ext.reshape(ext_rows // BF16_ROWS, BF16_ROWS, cblk)
            prods = [(ext * cwb_ref[k * BF16_ROWS:(k + 1) * BF16_ROWS, cols][None]
                      ).reshape(ext_rows, cblk) for k in range(SSD_CONV)]
            prods.append(jnp.zeros((shift_k - SSD_CONV * ext_rows, cblk), BF16))
            acc = _dot(shift_mat, jnp.concatenate(prods, axis=0)) + cb_ref[:, cols]
            tail_ref[:, cols] = xbc_ref[pl.ds(pl.multiple_of(ci * L + L - BF16_ROWS, BF16_ROWS),
                                              BF16_ROWS), cols]
            xcv_ref[:, cols] = acc * _sigmoid(acc)

        dtc = dt_ref[rows, :]
        da_hi, da_lo = _split_bf16(dtc * a_row)
        acum = _dot(tri, da_hi) + _dot(tri, da_lo)
        acum_t = acum.T
        both_hi, both_lo = _split_bf16(jnp.concatenate([dtc, acum], axis=0))
        ex = _dot(both_hi, expand) + _dot(both_lo, expand)
        dt_exp = ex[:L]
        ac_exp = ex[L:]
        a_last = ac_exp[L - 1:L, :]
        xdt = xcv_ref[:, :SSD_INNER] * dt_exp
        xdt_ref[...] = xdt.astype(BF16)
        xw_ref[...] = (xdt * jnp.exp(a_last - ac_exp)).astype(BF16)
        eac_ref[...] = jnp.exp(ac_exp)
        chunk_decay = jnp.exp(a_last)

        for g in range(SSD_GROUPS):
            gcols = slice(g * GROUP_WIDTH, (g + 1) * GROUP_WIDTH)
            b0 = SSD_INNER + g * SSD_STATE
            c0 = SSD_INNER + SSD_GROUPS * SSD_STATE + g * SSD_STATE
            bm_f = xcv_ref[:, b0:b0 + SSD_STATE]
            bm_b = bm_f.astype(BF16)
            bmt_b = bm_f.T.astype(BF16)
            cm_b = xcv_ref[:, c0:c0 + SSD_STATE].astype(BF16)
            cbm = lax.dot_general(cm_b, bm_b, (((1,), (1,)), ((), ())),
                                  preferred_element_type=F32)
            st = state_ref[g]
            y_off = _dot(cm_b, st.astype(BF16)) * eac_ref[:, gcols]
            state_ref[g] = st * chunk_decay[:, gcols] + _dot(bmt_b, xw_ref[:, gcols])

            y_pairs = []
            for jj in range(2):
                j = 2 * g + jj
                scores = []
                for h in (2 * j, 2 * j + 1):
                    seg = acum[:, h:h + 1] - acum_t[h:h + 1, :]
                    decay = jnp.exp(jnp.where(causal, seg, -jnp.inf))
                    scores.append((cbm * decay).astype(BF16))
                lhs = jnp.concatenate(scores, axis=1)
                xp = xdt_ref[:, j * LANES:(j + 1) * LANES]
                zero = jnp.zeros_like(xp)
                rhs = jnp.concatenate([jnp.where(low_half, xp, zero),
                                       jnp.where(low_half, zero, xp)], axis=0)
                y_pairs.append(_dot(lhs, rhs))
            y = (jnp.concatenate(y_pairs, axis=1) + y_off
                 + xcv_ref[:, gcols] * dskip_ref[:, gcols])
            zf = z_ref[rows, gcols].astype(F32)
            yz = y * (zf * _sigmoid(zf))
            ms = jnp.mean(yz * yz, axis=-1, keepdims=True)
            out_ref[rows, gcols] = (yz * lax.rsqrt(ms + EPS) * nw_ref[:, gcols]).astype(BF16)
        return carry

    lax.fori_loop(0, TS_SSD // L, chunk, 0, unroll=2)


def _ssd(fused, dt, conv_w, conv_b, a_log, d_skip, norm_w, bsz, seq, layer):
    ntok = fused.shape[0]
    nt = seq // TS_SSD
    row = lambda b, s: (b * nt + s, 0)
    return pl.pallas_call(
        _ssd_kernel,
        grid=(bsz, nt),
        in_specs=[
            pl.BlockSpec((TS_SSD, SSD_INNER), lambda b, s: (b * nt + s, Z_BLOCK)),
            pl.BlockSpec((TS_SSD, SSD_CONV_DIM), lambda b, s: (b * nt + s, XBC_BLOCK)),
            pl.BlockSpec((TS_SSD, LANES), row),
            _const_spec((SSD_CONV, SSD_CONV_DIM), layer),
            _const_spec((1, SSD_CONV_DIM), layer),
            _const_spec((1, LANES), layer),
            _const_spec((1, SSD_INNER), layer),
            _const_spec((1, SSD_INNER), layer),
        ],
        out_specs=pl.BlockSpec((TS_SSD, SSD_INNER), row),
        out_shape=jax.ShapeDtypeStruct((ntok, SSD_INNER), BF16),
        scratch_shapes=[
            pltpu.VMEM((SSD_GROUPS, SSD_STATE, GROUP_WIDTH), F32),
            pltpu.VMEM((BF16_ROWS, SSD_CONV_DIM), BF16),
            pltpu.VMEM((SSD_CONV * BF16_ROWS, SSD_CONV_DIM), BF16),
            pltpu.VMEM((SSD_CHUNK, SSD_CONV_DIM), F32),
            pltpu.VMEM((SSD_CHUNK, SSD_INNER), BF16),
            pltpu.VMEM((SSD_CHUNK, SSD_INNER), BF16),
            pltpu.VMEM((SSD_CHUNK, SSD_INNER), F32),
        ],
        compiler_params=pltpu.CompilerParams(
            dimension_semantics=("arbitrary", "arbitrary"), vmem_limit_bytes=VMEM_LIMIT),
        name="ssd",
    )(fused, fused, dt, conv_w, conv_b, a_log, d_skip, norm_w)


def _merge_kernel(yt_ref, ya_ref, gs_ref, bg_ref, x_ref, cb_ref, lng_ref, lnb_ref,
                  wa_ref, wb_ref, wo_ref, out_ref):
    c = yt_ref[...].astype(F32).T + cb_ref[...]
    mu = jnp.mean(c, axis=-1, keepdims=True)
    d = c - mu
    var = jnp.mean(d * d, axis=-1, keepdims=True)
    yb = d * lax.rsqrt(var + EPS) * lng_ref[...] + lnb_ref[...]
    yb = (yb * _sigmoid(yb)).astype(BF16)
    pa = _dot(ya_ref[...], wa_ref[...])
    pb = _dot(yb, wb_ref[...])
    ga = _sigmoid(gs_ref[:, :D_MODEL].astype(F32) + bg_ref[:, :D_MODEL])
    gb = _sigmoid(gs_ref[:, D_MODEL:].astype(F32) + bg_ref[:, D_MODEL:])
    merged = ga * pa + gb * pb
    out_ref[...] = x_ref[...] + _dot(merged.astype(BF16), wo_ref[...])


def _merge(yt, ya, fused, b_gate, x2d, conv_b, ln_g, ln_b, w_a, w_b, w_o, layer):
    ntok = ya.shape[0]
    row = lambda i: (i, 0)
    return pl.pallas_call(
        _merge_kernel,
        grid=(ntok // TM_MERGE,),
        in_specs=[
            pl.BlockSpec((CM_DIM, TM_MERGE), lambda i: (0, i)),
            pl.BlockSpec((TM_MERGE, SSD_INNER), row),
            pl.BlockSpec((TM_MERGE, 2 * D_MODEL), lambda i: (i, GATES_BLOCK)),
            _const_spec((1, 2 * D_MODEL), layer),
            pl.BlockSpec((TM_MERGE, D_MODEL), row),
            _const_spec((1, CM_DIM), layer),
            _const_spec((1, CM_DIM), layer),
            _const_spec((1, CM_DIM), layer),
            _const_spec((SSD_INNER, D_MODEL), layer),
            _const_spec((CM_DIM, D_MODEL), layer),
            _const_spec((D_MODEL, D_MODEL), layer),
        ],
        out_specs=pl.BlockSpec((TM_MERGE, D_MODEL), row),
        out_shape=jax.ShapeDtypeStruct((ntok, D_MODEL), F32),
        compiler_params=pltpu.CompilerParams(
            dimension_semantics=("arbitrary",), vmem_limit_bytes=VMEM_LIMIT),
        name="merge",
    )(yt, ya, fused, b_gate, x2d, conv_b, ln_g, ln_b, w_a, w_b, w_o)


def _ffn_kernel(x_ref, p_ref, gf_ref, wup_ref, fcw_ref, fcb_ref, wdn_ref, gp_ref, wpg_ref,
                wpe_ref, gfin_ref, out_ref, h_ref, tail_ref, acc_ref, *, final):
    @pl.when(pl.program_id(1) == 0)
    def _():
        tail_ref[...] = jnp.zeros_like(tail_ref)

    xf = x_ref[...]
    h_ref[...] = _rms(xf, gf_ref[...]).astype(BF16)
    acc_ref[...] = xf

    def up_cols(c, half):
        return slice(half * FFN_DIM + c * CW_FFN, half * FFN_DIM + (c + 1) * CW_FFN)

    def up(c):
        return [_dot(h_ref[...], wup_ref[:, up_cols(c, half)]) for half in range(2)]

    n_chunks = FFN_DIM // CW_FFN
    pending = [up(c) for c in range(FFN_LOOKAHEAD)]
    for c in range(n_chunks):
        if c + FFN_LOOKAHEAD < n_chunks:
            pending.append(up(c + FFN_LOOKAHEAD))
        current = pending.pop(0)
        halves = []
        for half, u in enumerate(current):
            cols = up_cols(c, half)
            ext = jnp.concatenate([tail_ref[:, cols], u], axis=0)
            y = fcb_ref[:, cols] + fcw_ref[FFN_CONV - 1:FFN_CONV, cols] * u
            for sft in range(1, FFN_CONV):
                k = FFN_CONV - 1 - sft
                y = y + fcw_ref[k:k + 1, cols] * _shift_rows(ext, sft)
            tail_ref[:, cols] = u[TS_FFN - SUBLANES:, :]
            halves.append(y)
        gate, val = halves
        act = (gate * _sigmoid(gate) * val).astype(BF16)
        acc_ref[...] += _dot(act, wdn_ref[c * CW_FFN:(c + 1) * CW_FFN, :])
    x1 = acc_ref[...]
    h2 = _rms(x1, gp_ref[...]).astype(BF16)
    pg = _sigmoid(_dot(h2, wpg_ref[...]))
    pe = _dot(p_ref[...].astype(BF16), wpe_ref[...])
    x2 = x1 + pg * pe
    if final:
        x2 = _rms(x2, gfin_ref[...])
    out_ref[...] = x2


def _ffn(x2d, p3d, g_ffn, w_up, conv_w, conv_b, w_down, g_ple, w_pg, w_pe, g_final,
         bsz, seq, layer, final):
    ntok = x2d.shape[0]
    nt = seq // TS_FFN
    row = lambda b, s: (b * nt + s, 0)
    return pl.pallas_call(
        functools.partial(_ffn_kernel, final=final),
        grid=(bsz, nt),
        in_specs=[
            pl.BlockSpec((TS_FFN, D_MODEL), row),
            pl.BlockSpec((None, TS_FFN, PLE_DIM), lambda b, s: (layer, b * nt + s, 0)),
            _const_spec((1, D_MODEL), layer),
            _const_spec((D_MODEL, 2 * FFN_DIM), layer),
            _const_spec((FFN_CONV, 2 * FFN_DIM), layer),
            _const_spec((1, 2 * FFN_DIM), layer),
            _const_spec((FFN_DIM, D_MODEL), layer),
            _const_spec((1, D_MODEL), layer),
            _const_spec((D_MODEL, D_MODEL), layer),
            _const_spec((PLE_DIM, D_MODEL), layer),
            _const_spec((1, D_MODEL)),
        ],
        out_specs=pl.BlockSpec((TS_FFN, D_MODEL), row),
        out_shape=jax.ShapeDtypeStruct((ntok, D_MODEL), F32),
        scratch_shapes=[
            pltpu.VMEM((TS_FFN, D_MODEL), BF16),
            pltpu.VMEM((SUBLANES, 2 * FFN_DIM), F32),
            pltpu.VMEM((TS_FFN, D_MODEL), F32),
        ],
        compiler_params=pltpu.CompilerParams(
            dimension_semantics=("arbitrary", "arbitrary"), vmem_limit_bytes=VMEM_LIMIT),
        name="ffn_ple",
    )(x2d, p3d, g_ffn, w_up, conv_w, conv_b, w_down, g_ple, w_pg, w_pe, g_final)


def _pad_lanes(a, width):
    return jnp.pad(a, [(0, 0)] * (a.ndim - 1) + [(0, width - a.shape[-1])])


def _rows(a):
    return a[:, None, :]


def kernel(x, p, norm_mix, w_in, b_gate, ssd_conv_w, ssd_conv_b, dt_bias, a_log, d_skip, ssd_norm, w_ssd_out, cm_conv_w, cm_conv_b, cm_ln_g, cm_ln_b, w_cm_out, w_out, norm_ffn, w_up, ffn_conv_w, ffn_conv_b, w_down, norm_ple, w_ple_gate, w_ple_proj, final_norm):
    bsz, seq, d = x.shape
    depth = w_in.shape[0]
    ntok = bsz * seq
    dt0 = SSD_INNER + SSD_CONV_DIM
    val0 = dt0 + SSD_HEADS
    gs0 = val0 + 2 * CM_DIM
    w_all = jnp.concatenate([w_in[:, :, val0:gs0], w_in[:, :, SSD_INNER:dt0], w_in[:, :, gs0:],
                             w_in[:, :, :SSD_INNER]], axis=2).astype(BF16)
    w_dt = _pad_lanes(w_in[:, :, dt0:val0], LANES).astype(BF16)
    tw = _toeplitz(_pad_lanes(jnp.swapaxes(cm_conv_w, 1, 2), LANES).astype(BF16))
    w_ssd_out_b, w_cm_out_b, w_out_b = _to_bf16(w_ssd_out), _to_bf16(w_cm_out), _to_bf16(w_out)
    w_up_b, w_down_b = _to_bf16(w_up), _to_bf16(w_down)
    w_pg_b, w_pe_b = _to_bf16(w_ple_gate), _to_bf16(w_ple_proj)
    norm_mix_r, dt_bias_r = _rows(norm_mix), _rows(_pad_lanes(dt_bias, LANES))
    ssd_conv_b_r, a_log_r = _rows(ssd_conv_b), _rows(_pad_lanes(a_log, LANES))
    d_skip_r, ssd_norm_r = _rows(jnp.repeat(d_skip, SSD_HEAD_DIM, axis=1)), _rows(ssd_norm)
    b_gate_r = b_gate.reshape(depth, 1, -1)
    cm_conv_b_r, cm_ln_g_r, cm_ln_b_r = _rows(cm_conv_b), _rows(cm_ln_g), _rows(cm_ln_b)
    norm_ffn_r, ffn_conv_b_r, norm_ple_r = _rows(norm_ffn), _rows(ffn_conv_b), _rows(norm_ple)
    p3d = p.reshape(depth, ntok, PLE_DIM)

    x2d = x.reshape(ntok, d)
    for i in range(depth):
        fused, vt, dt = _inproj(x2d, norm_mix_r, w_all, w_dt, dt_bias_r, i)
        yt = _conv31(vt, tw, seq, i)
        ya = _ssd(fused, dt, ssd_conv_w, ssd_conv_b_r, a_log_r, d_skip_r, ssd_norm_r, bsz, seq, i)
        x2d = _merge(yt, ya, fused, b_gate_r, x2d, cm_conv_b_r, cm_ln_g_r, cm_ln_b_r,
                     w_ssd_out_b, w_cm_out_b, w_out_b, i)
        x2d = _ffn(x2d, p3d, norm_ffn_r, w_up_b, ffn_conv_w, ffn_conv_b_r, w_down_b, norm_ple_r,
                   w_pg_b, w_pe_b, final_norm[None], bsz, seq, i, final=(i == depth - 1))
    return x2d.reshape(bsz, seq, d)
```

```python
import functools

import jax
import jax.numpy as jnp
from jax import lax
from jax.experimental import pallas as pl
from jax.experimental.pallas import tpu as pltpu

F32 = jnp.float32
BF16 = jnp.bfloat16

D_MODEL = 1024
PLE_DIM = 256
SSD_HEADS = 16
SSD_HEAD_DIM = 64
SSD_INNER = SSD_HEADS * SSD_HEAD_DIM
SSD_GROUPS = 4
SSD_STATE = 128
SSD_CONV = 4
SSD_CHUNK = 128
SSD_CONV_DIM = SSD_INNER + 2 * SSD_GROUPS * SSD_STATE
GROUP_WIDTH = SSD_INNER // SSD_GROUPS
CM_DIM = D_MODEL
CM_KERNEL = 31
FFN_DIM = 2816
FFN_CONV = 3
EPS = 1e-6

LANES = 128
SUBLANES = 8
BF16_ROWS = 16
VMEM_LIMIT = 56 * 1024 * 1024

TM_IN = 512
NC_IN = 512
FUSED_DIM = SSD_CONV_DIM + 2 * D_MODEL + SSD_INNER
XBC_BLOCK = 0
GATES_BLOCK = SSD_CONV_DIM // (2 * D_MODEL)
Z_BLOCK = (SSD_CONV_DIM + 2 * D_MODEL) // SSD_INNER
TS_SSD = 512
TM_MERGE = 1024
CONV_BLOCK = 128
CONV_CHANNELS = 32
CONV_UNROLL = 8
TS_FFN = 256
CW_FFN = 256
CAST_ROWS = 256
FFN_LOOKAHEAD = 3


def _sigmoid(x):
    return jax.nn.sigmoid(x)


def _rms(xf, g):
    ms = jnp.mean(xf * xf, axis=-1, keepdims=True)
    return xf * lax.rsqrt(ms + EPS) * g


def _dot(a, b):
    return jnp.dot(a, b, preferred_element_type=F32)


def _split_bf16(x):
    hi = x.astype(BF16)
    lo = (x - hi.astype(F32)).astype(BF16)
    return hi, lo


def _shift_rows(ext, shift):
    return pltpu.roll(ext, shift, 0)[SUBLANES:, :]


def _const_spec(shape, layer=None):
    nd = len(shape)
    if layer is None:
        return pl.BlockSpec(shape, lambda *_: (0,) * nd, pipeline_mode=pl.Buffered(1))
    return pl.BlockSpec((None,) + shape, lambda *_: (layer,) + (0,) * nd,
                        pipeline_mode=pl.Buffered(1))


def _cast_kernel(w_ref, o_ref):
    o_ref[...] = w_ref[...].astype(BF16)


def _to_bf16(w):
    nl, r, c = w.shape
    rb = min(r, CAST_ROWS)
    spec = pl.BlockSpec((None, rb, c), lambda l, i: (l, i, 0))
    return pl.pallas_call(
        _cast_kernel,
        grid=(nl, r // rb),
        in_specs=[spec],
        out_specs=spec,
        out_shape=jax.ShapeDtypeStruct(w.shape, BF16),
        compiler_params=pltpu.CompilerParams(
            dimension_semantics=("arbitrary", "arbitrary"), vmem_limit_bytes=VMEM_LIMIT),
        name="cast_bf16",
    )(w)


def _inproj_kernel(x_ref, g_ref, w_ref, wdt_ref, dtb_ref, fused_ref, vt_ref, dt_ref, h_ref):
    h_ref[...] = _rms(x_ref[...], g_ref[...]).astype(BF16)

    def mm(c0):
        return _dot(h_ref[...], w_ref[:, c0:c0 + NC_IN])

    for j in range(0, CM_DIM, NC_IN):
        v = mm(j) * _sigmoid(mm(CM_DIM + j))
        vt_ref[j:j + NC_IN, :] = v.T.astype(BF16)
    for j in range(0, FUSED_DIM, NC_IN):
        fused_ref[:, j:j + NC_IN] = mm(2 * CM_DIM + j).astype(BF16)
    dtr = _dot(h_ref[...], wdt_ref[...]) + dtb_ref[...]
    dt = jnp.maximum(dtr, 0.0) + jnp.log1p(jnp.exp(-jnp.abs(dtr)))
    lane = lax.broadcasted_iota(jnp.int32, dt.shape, 1)
    dt_ref[...] = jnp.where(lane < SSD_HEADS, dt, 0.0)


def _inproj(x2d, g, w_all, w_dt, dt_bias, layer):
    ntok = x2d.shape[0]
    row = lambda i: (i, 0)
    return pl.pallas_call(
        _inproj_kernel,
        grid=(ntok // TM_IN,),
        in_specs=[
            pl.BlockSpec((TM_IN, D_MODEL), row),
            _const_spec((1, D_MODEL), layer),
            _const_spec((D_MODEL, 2 * CM_DIM + FUSED_DIM), layer),
            _const_spec((D_MODEL, LANES), layer),
            _const_spec((1, LANES), layer),
        ],
        out_specs=[
            pl.BlockSpec((TM_IN, FUSED_DIM), row),
            pl.BlockSpec((CM_DIM, TM_IN), lambda i: (0, i)),
            pl.BlockSpec((TM_IN, LANES), row),
        ],
        out_shape=[
            jax.ShapeDtypeStruct((ntok, FUSED_DIM), BF16),
            jax.ShapeDtypeStruct((CM_DIM, ntok), BF16),
            jax.ShapeDtypeStruct((ntok, LANES), F32),
        ],
        scratch_shapes=[pltpu.VMEM((TM_IN, D_MODEL), BF16)],
        compiler_params=pltpu.CompilerParams(
            dimension_semantics=("arbitrary",), vmem_limit_bytes=VMEM_LIMIT),
        name="inproj",
    )(x2d, g, w_all, w_dt, dt_bias)


def _toeplitz_kernel(wt_ref, out_ref):
    ncol = out_ref.shape[1]
    q = pl.program_id(1) * ncol + lax.broadcasted_iota(jnp.int32, (LANES, ncol), 1)
    k = lax.broadcasted_iota(jnp.int32, (LANES, ncol), 0)
    s = jnp.right_shift(q, 8)
    n = jnp.bitwise_and(q, 2 * CONV_BLOCK - 1)
    lag = jnp.bitwise_and(n, CONV_BLOCK - 1) - s + jnp.bitwise_and(n, CONV_BLOCK)
    onehot = (k == (CM_KERNEL - 1) - lag).astype(BF16)
    out_ref[...] = _dot(wt_ref[...], onehot).astype(BF16)


def _toeplitz(wt):
    nl = wt.shape[0]
    ncol = 2048
    total = CONV_BLOCK * 2 * CONV_BLOCK
    tw = pl.pallas_call(
        _toeplitz_kernel,
        grid=(nl, total // ncol),
        in_specs=[pl.BlockSpec((None, CM_DIM, LANES), lambda l, i: (l, 0, 0))],
        out_specs=pl.BlockSpec((None, CM_DIM, ncol), lambda l, i: (l, 0, i)),
        out_shape=jax.ShapeDtypeStruct((nl, CM_DIM, total), BF16),
        compiler_params=pltpu.CompilerParams(
            dimension_semantics=("arbitrary", "arbitrary"), vmem_limit_bytes=VMEM_LIMIT),
        name="toeplitz",
    )(wt)
    return tw.reshape(nl, CM_DIM, CONV_BLOCK, 2 * CONV_BLOCK)


def _conv31_kernel(vt_ref, tw_ref, yt_ref, *, blocks_per_seq):
    nb = vt_ref.shape[1]
    blk = lax.broadcasted_iota(jnp.int32, (nb, CONV_BLOCK), 0)
    has_prev = jnp.bitwise_and(blk, blocks_per_seq - 1) != 0

    def channels(i, carry):
        for u in range(CONV_UNROLL):
            c = i * CONV_UNROLL + u
            z = _dot(vt_ref[c], tw_ref[c])
            from_prev = pltpu.roll(z[:, CONV_BLOCK:], 1, 0)
            yt_ref[c] = z[:, :CONV_BLOCK] + jnp.where(has_prev, from_prev, 0.0)
        return carry

    lax.fori_loop(0, vt_ref.shape[0] // CONV_UNROLL, channels, 0)


def _conv31(vt, tw, seq, layer):
    nch, ntok = vt.shape
    nb = ntok // CONV_BLOCK
    blocks_per_seq = seq // CONV_BLOCK
    assert blocks_per_seq & (blocks_per_seq - 1) == 0
    spec = pl.BlockSpec((CONV_CHANNELS, nb, CONV_BLOCK), lambda i: (i, 0, 0))
    return pl.pallas_call(
        functools.partial(_conv31_kernel, blocks_per_seq=blocks_per_seq),
        grid=(nch // CONV_CHANNELS,),
        in_specs=[spec, pl.BlockSpec((None, CONV_CHANNELS, CONV_BLOCK, 2 * CONV_BLOCK),
                                     lambda i: (layer, i, 0, 0))],
        out_specs=spec,
        out_shape=jax.ShapeDtypeStruct((nch, nb, CONV_BLOCK), F32),
        compiler_params=pltpu.CompilerParams(
            dimension_semantics=("arbitrary",), vmem_limit_bytes=VMEM_LIMIT),
        name="conv31",
    )(vt.reshape(nch, nb, CONV_BLOCK), tw)


def _ssd_kernel(z_ref, xbc_ref, dt_ref, cw_ref, cb_ref, alog_ref, dskip_ref, nw_ref,
                out_ref, state_ref, tail_ref, cwb_ref, xcv_ref, xdt_ref, xw_ref, eac_ref):
    L = SSD_CHUNK

    @pl.when(pl.program_id(1) == 0)
    def _():
        state_ref[...] = jnp.zeros_like(state_ref)
        tail_ref[...] = jnp.zeros_like(tail_ref)

    for k in range(SSD_CONV):
        cwb_ref[k * BF16_ROWS:(k + 1) * BF16_ROWS, :] = jnp.broadcast_to(
            cw_ref[k:k + 1, :], (BF16_ROWS, SSD_CONV_DIM)).astype(BF16)

    li = lax.broadcasted_iota(jnp.int32, (L, L), 0)
    si = lax.broadcasted_iota(jnp.int32, (L, L), 1)
    causal = si <= li
    tri = causal.astype(BF16)
    hrow = lax.broadcasted_iota(jnp.int32, (LANES, SSD_INNER), 0)
    hcol = lax.broadcasted_iota(jnp.int32, (LANES, SSD_INNER), 1)
    head_of = jnp.right_shift(hcol, SSD_HEAD_DIM.bit_length() - 1)
    expand = (head_of == hrow).astype(BF16)
    expand2 = jnp.concatenate([expand, expand], axis=0)
    tri2 = jnp.concatenate([tri, tri], axis=1)
    low_half = si < SSD_HEAD_DIM
    ext_rows = BF16_ROWS + L
    shift_k = -(-SSD_CONV * ext_rows // LANES) * LANES
    st = lax.broadcasted_iota(jnp.int32, (L, shift_k), 0)
    sc = lax.broadcasted_iota(jnp.int32, (L, shift_k), 1)
    hit = sc < 0
    for k in range(SSD_CONV):
        hit = hit | (sc == st + (k * ext_rows + BF16_ROWS - (SSD_CONV - 1 - k)))
    shift_mat = hit.astype(BF16)
    a_row = -jnp.exp(alog_ref[...])

    def chunk(ci, carry):
        rows = pl.ds(pl.multiple_of(ci * L, L), L)

        cblk = 512
        for c0 in range(0, SSD_CONV_DIM, cblk):
            cols = slice(c0, c0 + cblk)
            ext = jnp.concatenate([tail_ref[:, cols], xbc_ref[rows, cols]], axis=0)
            ext = ext.reshape(ext_rows // BF16_ROWS, BF16_ROWS, cblk)
            prods = [(ext * cwb_ref[k * BF16_ROWS:(k + 1) * BF16_ROWS, cols][None]
                      ).reshape(ext_rows, cblk) for k in range(SSD_CONV)]
            prods.append(jnp.zeros((shift_k - SSD_CONV * ext_rows, cblk), BF16))
            acc = _dot(shift_mat, jnp.concatenate(prods, axis=0)) + cb_ref[:, cols]
            tail_ref[:, cols] = xbc_ref[pl.ds(pl.multiple_of(ci * L + L - BF16_ROWS, BF16_ROWS),
                                              BF16_ROWS), cols]
            xcv_ref[:, cols] = acc * _sigmoid(acc)

        dtc = dt_ref[rows, :]
        acum = _dot(tri2, jnp.concatenate(_split_bf16(dtc * a_row), axis=0))
        acum_t = acum.T
        both = jnp.concatenate(_split_bf16(jnp.concatenate([dtc, acum], axis=0)), axis=1)
        ex = _dot(both, expand2)
        dt_exp = ex[:L]
        ac_exp = ex[L:]
        a_last = ac_exp[L - 1:L, :]
        xdt = xcv_ref[:, :SSD_INNER] * dt_exp
        xdt_ref[...] = xdt.astype(BF16)
        xw_ref[...] = (xdt * jnp.exp(a_last - ac_exp)).astype(BF16)
        eac_ref[...] = jnp.exp(ac_exp)
        chunk_decay = jnp.exp(a_last)

        for g in range(SSD_GROUPS):
            gcols = slice(g * GROUP_WIDTH, (g + 1) * GROUP_WIDTH)
            b0 = SSD_INNER + g * SSD_STATE
            c0 = SSD_INNER + SSD_GROUPS * SSD_STATE + g * SSD_STATE
            bm_f = xcv_ref[:, b0:b0 + SSD_STATE]
            bm_b = bm_f.astype(BF16)
            bmt_b = bm_f.T.astype(BF16)
            cm_b = xcv_ref[:, c0:c0 + SSD_STATE].astype(BF16)
            cbm = lax.dot_general(cm_b, bm_b, (((1,), (1,)), ((), ())),
                                  preferred_element_type=F32)
            st = state_ref[g]
            y_off = _dot(cm_b, st.astype(BF16)) * eac_ref[:, gcols]
            state_ref[g] = st * chunk_decay[:, gcols] + _dot(bmt_b, xw_ref[:, gcols])

            y_pairs = []
            for jj in range(2):
                j = 2 * g + jj
                scores = []
                for h in (2 * j, 2 * j + 1):
                    seg = acum[:, h:h + 1] - acum_t[h:h + 1, :]
                    decay = jnp.exp(jnp.where(causal, seg, -jnp.inf))
                    scores.append((cbm * decay).astype(BF16))
                lhs = jnp.concatenate(scores, axis=1)
                xp = xdt_ref[:, j * LANES:(j + 1) * LANES]
                zero = jnp.zeros_like(xp)
                rhs = jnp.concatenate([jnp.where(low_half, xp, zero),
                                       jnp.where(low_half, zero, xp)], axis=0)
                y_pairs.append(_dot(lhs, rhs))
            y = (jnp.concatenate(y_pairs, axis=1) + y_off
                 + xcv_ref[:, gcols] * dskip_ref[:, gcols])
            zf = z_ref[rows, gcols].astype(F32)
            yz = y * (zf * _sigmoid(zf))
            ms = jnp.mean(yz * yz, axis=-1, keepdims=True)
            out_ref[rows, gcols] = (yz * lax.rsqrt(ms + EPS) * nw_ref[:, gcols]).astype(BF16)
        return carry

    lax.fori_loop(0, TS_SSD // L, chunk, 0, unroll=True)


def _ssd(fused, dt, conv_w, conv_b, a_log, d_skip, norm_w, bsz, seq, layer):
    ntok = fused.shape[0]
    nt = seq // TS_SSD
    row = lambda b, s: (b * nt + s, 0)
    return pl.pallas_call(
        _ssd_kernel,
        grid=(bsz, nt),
        in_specs=[
            pl.BlockSpec((TS_SSD, SSD_INNER), lambda b, s: (b * nt + s, Z_BLOCK)),
            pl.BlockSpec((TS_SSD, SSD_CONV_DIM), lambda b, s: (b * nt + s, XBC_BLOCK)),
            pl.BlockSpec((TS_SSD, LANES), row),
            _const_spec((SSD_CONV, SSD_CONV_DIM), layer),
            _const_spec((1, SSD_CONV_DIM), layer),
            _const_spec((1, LANES), layer),
            _const_spec((1, SSD_INNER), layer),
            _const_spec((1, SSD_INNER), layer),
        ],
        out_specs=pl.BlockSpec((TS_SSD, SSD_INNER), row),
        out_shape=jax.ShapeDtypeStruct((ntok, SSD_INNER), BF16),
        scratch_shapes=[
            pltpu.VMEM((SSD_GROUPS, SSD_STATE, GROUP_WIDTH), F32),
            pltpu.VMEM((BF16_ROWS, SSD_CONV_DIM), BF16),
            pltpu.VMEM((SSD_CONV * BF16_ROWS, SSD_CONV_DIM), BF16),
            pltpu.VMEM((SSD_CHUNK, SSD_CONV_DIM), F32),
            pltpu.VMEM((SSD_CHUNK, SSD_INNER), BF16),
            pltpu.VMEM((SSD_CHUNK, SSD_INNER), BF16),
            pltpu.VMEM((SSD_CHUNK, SSD_INNER), F32),
        ],
        compiler_params=pltpu.CompilerParams(
            dimension_semantics=("arbitrary", "arbitrary"), vmem_limit_bytes=VMEM_LIMIT),
        name="ssd",
    )(fused, fused, dt, conv_w, conv_b, a_log, d_skip, norm_w)


def _merge_kernel(yt_ref, ya_ref, gs_ref, bg_ref, x_ref, cb_ref, lng_ref, lnb_ref,
                  wa_ref, wb_ref, wo_ref, out_ref):
    c = jnp.concatenate([yt_ref[:, j, :].T for j in range(TM_MERGE // CONV_BLOCK)], axis=0)
    c = c + cb_ref[...]
    mu = jnp.mean(c, axis=-1, keepdims=True)
    d = c - mu
    var = jnp.mean(d * d, axis=-1, keepdims=True)
    yb = d * lax.rsqrt(var + EPS) * lng_ref[...] + lnb_ref[...]
    yb = (yb * _sigmoid(yb)).astype(BF16)
    pa = _dot(ya_ref[...], wa_ref[...])
    pb = _dot(yb, wb_ref[...])
    ga = _sigmoid(gs_ref[:, :D_MODEL].astype(F32) + bg_ref[:, :D_MODEL])
    gb = _sigmoid(gs_ref[:, D_MODEL:].astype(F32) + bg_ref[:, D_MODEL:])
    merged = ga * pa + gb * pb
    out_ref[...] = x_ref[...] + _dot(merged.astype(BF16), wo_ref[...])


def _merge(yt, ya, fused, b_gate, x2d, conv_b, ln_g, ln_b, w_a, w_b, w_o, layer):
    ntok = ya.shape[0]
    row = lambda i: (i, 0)
    return pl.pallas_call(
        _merge_kernel,
        grid=(ntok // TM_MERGE,),
        in_specs=[
            pl.BlockSpec((CM_DIM, TM_MERGE // CONV_BLOCK, CONV_BLOCK), lambda i: (0, i, 0)),
            pl.BlockSpec((TM_MERGE, SSD_INNER), row),
            pl.BlockSpec((TM_MERGE, 2 * D_MODEL), lambda i: (i, GATES_BLOCK)),
            _const_spec((1, 2 * D_MODEL), layer),
            pl.BlockSpec((TM_MERGE, D_MODEL), row),
            _const_spec((1, CM_DIM), layer),
            _const_spec((1, CM_DIM), layer),
            _const_spec((1, CM_DIM), layer),
            _const_spec((SSD_INNER, D_MODEL), layer),
            _const_spec((CM_DIM, D_MODEL), layer),
            _const_spec((D_MODEL, D_MODEL), layer),
        ],
        out_specs=pl.BlockSpec((TM_MERGE, D_MODEL), row),
        out_shape=jax.ShapeDtypeStruct((ntok, D_MODEL), F32),
        compiler_params=pltpu.CompilerParams(
            dimension_semantics=("arbitrary",), vmem_limit_bytes=VMEM_LIMIT),
        name="merge",
    )(yt, ya, fused, b_gate, x2d, conv_b, ln_g, ln_b, w_a, w_b, w_o)


def _ffn_kernel(x_ref, p_ref, gf_ref, wup_ref, fcw_ref, fcb_ref, wdn_ref, gp_ref, wpg_ref,
                wpe_ref, gfin_ref, out_ref, h_ref, tail_ref, acc_ref, *, final):
    @pl.when(pl.program_id(1) == 0)
    def _():
        tail_ref[...] = jnp.zeros_like(tail_ref)

    xf = x_ref[...]
    h_ref[...] = _rms(xf, gf_ref[...]).astype(BF16)
    acc_ref[...] = xf

    def up_cols(c, half):
        return slice(half * FFN_DIM + c * CW_FFN, half * FFN_DIM + (c + 1) * CW_FFN)

    def up(c):
        return [_dot(h_ref[...], wup_ref[:, up_cols(c, half)]) for half in range(2)]

    n_chunks = FFN_DIM // CW_FFN
    pending = [up(c) for c in range(FFN_LOOKAHEAD)]
    for c in range(n_chunks):
        if c + FFN_LOOKAHEAD < n_chunks:
            pending.append(up(c + FFN_LOOKAHEAD))
        current = pending.pop(0)
        halves = []
        for half, u in enumerate(current):
            cols = up_cols(c, half)
            ext = jnp.concatenate([tail_ref[:, cols], u], axis=0)
            y = fcb_ref[:, cols] + fcw_ref[FFN_CONV - 1:FFN_CONV, cols] * u
            for sft in range(1, FFN_CONV):
                k = FFN_CONV - 1 - sft
                y = y + fcw_ref[k:k + 1, cols] * _shift_rows(ext, sft)
            tail_ref[:, cols] = u[TS_FFN - SUBLANES:, :]
            halves.append(y)
        gate, val = halves
        act = (gate * _sigmoid(gate) * val).astype(BF16)
        acc_ref[...] += _dot(act, wdn_ref[c * CW_FFN:(c + 1) * CW_FFN, :])
    x1 = acc_ref[...]
    h2 = _rms(x1, gp_ref[...]).astype(BF16)
    pg = _sigmoid(_dot(h2, wpg_ref[...]))
    pe = _dot(p_ref[...].astype(BF16), wpe_ref[...])
    x2 = x1 + pg * pe
    if final:
        x2 = _rms(x2, gfin_ref[...])
    out_ref[...] = x2


def _ffn(x2d, p3d, g_ffn, w_up, conv_w, conv_b, w_down, g_ple, w_pg, w_pe, g_final,
         bsz, seq, layer, final):
    ntok = x2d.shape[0]
    nt = seq // TS_FFN
    row = lambda b, s: (b * nt + s, 0)
    return pl.pallas_call(
        functools.partial(_ffn_kernel, final=final),
        grid=(bsz, nt),
        in_specs=[
            pl.BlockSpec((TS_FFN, D_MODEL), row),
            pl.BlockSpec((None, TS_FFN, PLE_DIM), lambda b, s: (layer, b * nt + s, 0)),
            _const_spec((1, D_MODEL), layer),
            _const_spec((D_MODEL, 2 * FFN_DIM), layer),
            _const_spec((FFN_CONV, 2 * FFN_DIM), layer),
            _const_spec((1, 2 * FFN_DIM), layer),
            _const_spec((FFN_DIM, D_MODEL), layer),
            _const_spec((1, D_MODEL), layer),
            _const_spec((D_MODEL, D_MODEL), layer),
            _const_spec((PLE_DIM, D_MODEL), layer),
            _const_spec((1, D_MODEL)),
        ],
        out_specs=pl.BlockSpec((TS_FFN, D_MODEL), row),
        out_shape=jax.ShapeDtypeStruct((ntok, D_MODEL), F32),
        scratch_shapes=[
            pltpu.VMEM((TS_FFN, D_MODEL), BF16),
            pltpu.VMEM((SUBLANES, 2 * FFN_DIM), F32),
            pltpu.VMEM((TS_FFN, D_MODEL), F32),
        ],
        compiler_params=pltpu.CompilerParams(
            dimension_semantics=("arbitrary", "arbitrary"), vmem_limit_bytes=VMEM_LIMIT),
        name="ffn_ple",
    )(x2d, p3d, g_ffn, w_up, conv_w, conv_b, w_down, g_ple, w_pg, w_pe, g_final)


def _pad_lanes(a, width):
    return jnp.pad(a, [(0, 0)] * (a.ndim - 1) + [(0, width - a.shape[-1])])


def _rows(a):
    return a[:, None, :]


def kernel(x, p, norm_mix, w_in, b_gate, ssd_conv_w, ssd_conv_b, dt_bias, a_log, d_skip, ssd_norm, w_ssd_out, cm_conv_w, cm_conv_b, cm_ln_g, cm_ln_b, w_cm_out, w_out, norm_ffn, w_up, ffn_conv_w, ffn_conv_b, w_down, norm_ple, w_ple_gate, w_ple_proj, final_norm):
    bsz, seq, d = x.shape
    depth = w_in.shape[0]
    ntok = bsz * seq
    dt0 = SSD_INNER + SSD_CONV_DIM
    val0 = dt0 + SSD_HEADS
    gs0 = val0 + 2 * CM_DIM
    w_all = jnp.concatenate([w_in[:, :, val0:gs0], w_in[:, :, SSD_INNER:dt0], w_in[:, :, gs0:],
                             w_in[:, :, :SSD_INNER]], axis=2).astype(BF16)
    w_dt = _pad_lanes(w_in[:, :, dt0:val0], LANES).astype(BF16)
    tw = _toeplitz(_pad_lanes(jnp.swapaxes(cm_conv_w, 1, 2), LANES).astype(BF16))
    w_ssd_out_b, w_cm_out_b, w_out_b = _to_bf16(w_ssd_out), _to_bf16(w_cm_out), _to_bf16(w_out)
    w_up_b, w_down_b = _to_bf16(w_up), _to_bf16(w_down)
    w_pg_b, w_pe_b = _to_bf16(w_ple_gate), _to_bf16(w_ple_proj)
    norm_mix_r, dt_bias_r = _rows(norm_mix), _rows(_pad_lanes(dt_bias, LANES))
    ssd_conv_b_r, a_log_r = _rows(ssd_conv_b), _rows(_pad_lanes(a_log, LANES))
    d_skip_r, ssd_norm_r = _rows(jnp.repeat(d_skip, SSD_HEAD_DIM, axis=1)), _rows(ssd_norm)
    b_gate_r = b_gate.reshape(depth, 1, -1)
    cm_conv_b_r, cm_ln_g_r, cm_ln_b_r = _rows(cm_conv_b), _rows(cm_ln_g), _rows(cm_ln_b)
    norm_ffn_r, ffn_conv_b_r, norm_ple_r = _rows(norm_ffn), _rows(ffn_conv_b), _rows(norm_ple)
    p3d = p.reshape(depth, ntok, PLE_DIM)

    x2d = x.reshape(ntok, d)
    for i in range(depth):
        fused, vt, dt = _inproj(x2d, norm_mix_r, w_all, w_dt, dt_bias_r, i)
        yt = _conv31(vt, tw, seq, i)
        ya = _ssd(fused, dt, ssd_conv_w, ssd_conv_b_r, a_log_r, d_skip_r, ssd_norm_r, bsz, seq, i)
        x2d = _merge(yt, ya, fused, b_gate_r, x2d, cm_conv_b_r, cm_ln_g_r, cm_ln_b_r,
                     w_ssd_out_b, w_cm_out_b, w_out_b, i)
        x2d = _ffn(x2d, p3d, norm_ffn_r, w_up_b, ffn_conv_w, ffn_conv_b_r, w_down_b, norm_ple_r,
                   w_pg_b, w_pe_b, final_norm[None], bsz, seq, i, final=(i == depth - 1))
    return x2d.reshape(bsz, seq, d)
```

```python
import functools

import jax
import jax.numpy as jnp
from jax import lax
from jax.experimental import pallas as pl
from jax.experimental.pallas import tpu as pltpu

F32 = jnp.float32
BF16 = jnp.bfloat16

D_MODEL = 1024
PLE_DIM = 256
SSD_HEADS = 16
SSD_HEAD_DIM = 64
SSD_INNER = SSD_HEADS * SSD_HEAD_DIM
SSD_GROUPS = 4
SSD_STATE = 128
SSD_CONV = 4
SSD_CHUNK = 128
SSD_CONV_DIM = SSD_INNER + 2 * SSD_GROUPS * SSD_STATE
GROUP_WIDTH = SSD_INNER // SSD_GROUPS
CM_DIM = D_MODEL
CM_KERNEL = 31
FFN_DIM = 2816
FFN_CONV = 3
EPS = 1e-6

LANES = 128
SUBLANES = 8
BF16_ROWS = 16
VMEM_LIMIT = 56 * 1024 * 1024

TM_IN = 512
NC_IN = 512
FUSED_DIM = SSD_CONV_DIM + 2 * D_MODEL + SSD_INNER
XBC_BLOCK = 0
GATES_BLOCK = SSD_CONV_DIM // (2 * D_MODEL)
Z_BLOCK = (SSD_CONV_DIM + 2 * D_MODEL) // SSD_INNER
TS_SSD = 512
TM_MERGE = 1024
CONV_BLOCK = 128
CONV_CHANNELS = 32
CONV_UNROLL = 8
TS_FFN = 256
CW_FFN = 256
CAST_BLOCK_BYTES = 12 * 1024 * 1024
PREP_ROWS = 128
FFN_LOOKAHEAD = 3


def _sigmoid(x):
    return jax.nn.sigmoid(x)


def _rms(xf, g):
    ms = jnp.mean(xf * xf, axis=-1, keepdims=True)
    return xf * lax.rsqrt(ms + EPS) * g


def _dot(a, b):
    return jnp.dot(a, b, preferred_element_type=F32)


def _split_bf16(x):
    hi = x.astype(BF16)
    lo = (x - hi.astype(F32)).astype(BF16)
    return hi, lo


def _shift_rows(ext, shift):
    return pltpu.roll(ext, shift, 0)[SUBLANES:, :]


def _const_spec(shape, layer=None):
    nd = len(shape)
    if layer is None:
        return pl.BlockSpec(shape, lambda *_: (0,) * nd, pipeline_mode=pl.Buffered(1))
    return pl.BlockSpec((None,) + shape, lambda *_: (layer,) + (0,) * nd,
                        pipeline_mode=pl.Buffered(1))


def _cast_kernel(w_ref, o_ref):
    o_ref[...] = w_ref[...].astype(BF16)


def _to_bf16(w):
    nl, r, c = w.shape
    rb = r
    while rb * c * w.dtype.itemsize > CAST_BLOCK_BYTES and rb % 2 == 0:
        rb //= 2
    spec = pl.BlockSpec((None, rb, c), lambda l, i: (l, i, 0))
    return pl.pallas_call(
        _cast_kernel,
        grid=(nl, r // rb),
        in_specs=[spec],
        out_specs=spec,
        out_shape=jax.ShapeDtypeStruct(w.shape, BF16),
        compiler_params=pltpu.CompilerParams(
            dimension_semantics=("arbitrary", "arbitrary"), vmem_limit_bytes=VMEM_LIMIT),
        name="cast_bf16",
    )(w)


def _inproj_kernel(x_ref, g_ref, w_ref, wdt_ref, dtb_ref, fused_ref, vt_ref, dt_ref, h_ref):
    h_ref[...] = _rms(x_ref[...], g_ref[...]).astype(BF16)

    def mm(c0):
        return _dot(h_ref[...], w_ref[:, c0:c0 + NC_IN])

    for j in range(0, CM_DIM, NC_IN):
        v = mm(j) * _sigmoid(mm(CM_DIM + j))
        vt_ref[j:j + NC_IN, :] = v.T.astype(BF16)
    for j in range(0, FUSED_DIM, NC_IN):
        fused_ref[:, j:j + NC_IN] = mm(2 * CM_DIM + j).astype(BF16)
    dtr = _dot(h_ref[...], wdt_ref[...]) + dtb_ref[...]
    dt = jnp.maximum(dtr, 0.0) + jnp.log1p(jnp.exp(-jnp.abs(dtr)))
    lane = lax.broadcasted_iota(jnp.int32, dt.shape, 1)
    dt_ref[...] = jnp.where(lane < SSD_HEADS, dt, 0.0)


def _inproj(x2d, g, w_all, w_dt, dt_bias, layer):
    ntok = x2d.shape[0]
    row = lambda i: (i, 0)
    return pl.pallas_call(
        _inproj_kernel,
        grid=(ntok // TM_IN,),
        in_specs=[
            pl.BlockSpec((TM_IN, D_MODEL), row),
            _const_spec((1, D_MODEL), layer),
            _const_spec((D_MODEL, 2 * CM_DIM + FUSED_DIM), layer),
            _const_spec((D_MODEL, LANES), layer),
            _const_spec((1, LANES), layer),
        ],
        out_specs=[
            pl.BlockSpec((TM_IN, FUSED_DIM), row),
            pl.BlockSpec((CM_DIM, TM_IN), lambda i: (0, i)),
            pl.BlockSpec((TM_IN, LANES), row),
        ],
        out_shape=[
            jax.ShapeDtypeStruct((ntok, FUSED_DIM), BF16),
            jax.ShapeDtypeStruct((CM_DIM, ntok), BF16),
            jax.ShapeDtypeStruct((ntok, LANES), F32),
        ],
        scratch_shapes=[pltpu.VMEM((TM_IN, D_MODEL), BF16)],
        compiler_params=pltpu.CompilerParams(
            dimension_semantics=("arbitrary",), vmem_limit_bytes=VMEM_LIMIT),
        name="inproj",
    )(x2d, g, w_all, w_dt, dt_bias)


def _toeplitz_kernel(wt_ref, out_ref):
    ncol = out_ref.shape[1]
    q = pl.program_id(1) * ncol + lax.broadcasted_iota(jnp.int32, (LANES, ncol), 1)
    k = lax.broadcasted_iota(jnp.int32, (LANES, ncol), 0)
    s = jnp.right_shift(q, 8)
    n = jnp.bitwise_and(q, 2 * CONV_BLOCK - 1)
    lag = jnp.bitwise_and(n, CONV_BLOCK - 1) - s + jnp.bitwise_and(n, CONV_BLOCK)
    onehot = (k == (CM_KERNEL - 1) - lag).astype(BF16)
    out_ref[...] = _dot(wt_ref[...], onehot).astype(BF16)


def _toeplitz(wt):
    nl = wt.shape[0]
    ncol = 2048
    total = CONV_BLOCK * 2 * CONV_BLOCK
    tw = pl.pallas_call(
        _toeplitz_kernel,
        grid=(nl, total // ncol),
        in_specs=[pl.BlockSpec((None, CM_DIM, LANES), lambda l, i: (l, 0, 0))],
        out_specs=pl.BlockSpec((None, CM_DIM, ncol), lambda l, i: (l, 0, i)),
        out_shape=jax.ShapeDtypeStruct((nl, CM_DIM, total), BF16),
        compiler_params=pltpu.CompilerParams(
            dimension_semantics=("arbitrary", "arbitrary"), vmem_limit_bytes=VMEM_LIMIT),
        name="toeplitz",
    )(wt)
    return tw.reshape(nl, CM_DIM, CONV_BLOCK, 2 * CONV_BLOCK)


def _conv31_kernel(vt_ref, tw_ref, yt_ref, *, blocks_per_seq):
    nb = vt_ref.shape[1]
    blk = lax.broadcasted_iota(jnp.int32, (nb, CONV_BLOCK), 0)
    has_prev = jnp.bitwise_and(blk, blocks_per_seq - 1) != 0

    def channels(i, carry):
        for u in range(CONV_UNROLL):
            c = i * CONV_UNROLL + u
            z = _dot(vt_ref[c], tw_ref[c])
            from_prev = pltpu.roll(z[:, CONV_BLOCK:], 1, 0)
            yt_ref[c] = z[:, :CONV_BLOCK] + jnp.where(has_prev, from_prev, 0.0)
        return carry

    lax.fori_loop(0, vt_ref.shape[0] // CONV_UNROLL, channels, 0)


def _conv31(vt, tw, seq, layer):
    nch, ntok = vt.shape
    nb = ntok // CONV_BLOCK
    blocks_per_seq = seq // CONV_BLOCK
    assert blocks_per_seq & (blocks_per_seq - 1) == 0
    spec = pl.BlockSpec((CONV_CHANNELS, nb, CONV_BLOCK), lambda i: (i, 0, 0))
    return pl.pallas_call(
        functools.partial(_conv31_kernel, blocks_per_seq=blocks_per_seq),
        grid=(nch // CONV_CHANNELS,),
        in_specs=[spec, pl.BlockSpec((None, CONV_CHANNELS, CONV_BLOCK, 2 * CONV_BLOCK),
                                     lambda i: (layer, i, 0, 0))],
        out_specs=spec,
        out_shape=jax.ShapeDtypeStruct((nch, nb, CONV_BLOCK), F32),
        compiler_params=pltpu.CompilerParams(
            dimension_semantics=("arbitrary",), vmem_limit_bytes=VMEM_LIMIT),
        name="conv31",
    )(vt.reshape(nch, nb, CONV_BLOCK), tw)


def _ssd_kernel(z_ref, xbc_ref, dt_ref, cw_ref, cb_ref, alog_ref, dskip_ref, nw_ref,
                out_ref, state_ref, tail_ref, cwb_ref, xcv_ref, xdt_ref, xw_ref, eac_ref):
    L = SSD_CHUNK

    @pl.when(pl.program_id(1) == 0)
    def _():
        state_ref[...] = jnp.zeros_like(state_ref)
        tail_ref[...] = jnp.zeros_like(tail_ref)

    for k in range(SSD_CONV):
        cwb_ref[k * BF16_ROWS:(k + 1) * BF16_ROWS, :] = jnp.broadcast_to(
            cw_ref[k:k + 1, :], (BF16_ROWS, SSD_CONV_DIM)).astype(BF16)

    li = lax.broadcasted_iota(jnp.int32, (L, L), 0)
    si = lax.broadcasted_iota(jnp.int32, (L, L), 1)
    causal = si <= li
    tri = causal.astype(BF16)
    hrow = lax.broadcasted_iota(jnp.int32, (LANES, SSD_INNER), 0)
    hcol = lax.broadcasted_iota(jnp.int32, (LANES, SSD_INNER), 1)
    head_of = jnp.right_shift(hcol, SSD_HEAD_DIM.bit_length() - 1)
    expand = (head_of == hrow).astype(BF16)
    expand2 = jnp.concatenate([expand, expand], axis=0)
    tri2 = jnp.concatenate([tri, tri], axis=1)
    low_half = si < SSD_HEAD_DIM
    ext_rows = BF16_ROWS + L
    shift_k = -(-SSD_CONV * ext_rows // LANES) * LANES
    st = lax.broadcasted_iota(jnp.int32, (L, shift_k), 0)
    sc = lax.broadcasted_iota(jnp.int32, (L, shift_k), 1)
    hit = sc < 0
    for k in range(SSD_CONV):
        hit = hit | (sc == st + (k * ext_rows + BF16_ROWS - (SSD_CONV - 1 - k)))
    shift_mat = hit.astype(BF16)
    a_row = -jnp.exp(alog_ref[...])

    def chunk(ci, carry):
        rows = pl.ds(pl.multiple_of(ci * L, L), L)

        cblk = 512
        for c0 in range(0, SSD_CONV_DIM, cblk):
            cols = slice(c0, c0 + cblk)
            ext = jnp.concatenate([tail_ref[:, cols], xbc_ref[rows, cols]], axis=0)
            ext = ext.reshape(ext_rows // BF16_ROWS, BF16_ROWS, cblk)
            prods = [(ext * cwb_ref[k * BF16_ROWS:(k + 1) * BF16_ROWS, cols][None]
                      ).reshape(ext_rows, cblk) for k in range(SSD_CONV)]
            prods.append(jnp.zeros((shift_k - SSD_CONV * ext_rows, cblk), BF16))
            acc = _dot(shift_mat, jnp.concatenate(prods, axis=0)) + cb_ref[:, cols]
            tail_ref[:, cols] = xbc_ref[pl.ds(pl.multiple_of(ci * L + L - BF16_ROWS, BF16_ROWS),
                                              BF16_ROWS), cols]
            xcv_ref[:, cols] = acc * _sigmoid(acc)

        dtc = dt_ref[rows, :]
        acum = _dot(tri2, jnp.concatenate(_split_bf16(dtc * a_row), axis=0))
        acum_t = acum.T
        both = jnp.concatenate(_split_bf16(jnp.concatenate([dtc, acum], axis=0)), axis=1)
        ex = _dot(both, expand2)
        dt_exp = ex[:L]
        ac_exp = ex[L:]
        a_last = ac_exp[L - 1:L, :]
        xdt = xcv_ref[:, :SSD_INNER] * dt_exp
        xdt_ref[...] = xdt.astype(BF16)
        xw_ref[...] = (xdt * jnp.exp(a_last - ac_exp)).astype(BF16)
        eac_ref[...] = jnp.exp(ac_exp)
        chunk_decay = jnp.exp(a_last)

        for g in range(SSD_GROUPS):
            gcols = slice(g * GROUP_WIDTH, (g + 1) * GROUP_WIDTH)
            b0 = SSD_INNER + g * SSD_STATE
            c0 = SSD_INNER + SSD_GROUPS * SSD_STATE + g * SSD_STATE
            bm_f = xcv_ref[:, b0:b0 + SSD_STATE]
            bm_b = bm_f.astype(BF16)
            bmt_b = bm_f.T.astype(BF16)
            cm_b = xcv_ref[:, c0:c0 + SSD_STATE].astype(BF16)
            cbm = lax.dot_general(cm_b, bm_b, (((1,), (1,)), ((), ())),
                                  preferred_element_type=F32)
            st = state_ref[g]
            y_off = _dot(cm_b, st.astype(BF16)) * eac_ref[:, gcols]
            state_ref[g] = st * chunk_decay[:, gcols] + _dot(bmt_b, xw_ref[:, gcols])

            y_pairs = []
            for jj in range(2):
                j = 2 * g + jj
                scores = []
                for h in (2 * j, 2 * j + 1):
                    seg = acum[:, h:h + 1] - acum_t[h:h + 1, :]
                    decay = jnp.exp(jnp.where(causal, seg, -jnp.inf))
                    scores.append((cbm * decay).astype(BF16))
                lhs = jnp.concatenate(scores, axis=1)
                xp = xdt_ref[:, j * LANES:(j + 1) * LANES]
                zero = jnp.zeros_like(xp)
                rhs = jnp.concatenate([jnp.where(low_half, xp, zero),
                                       jnp.where(low_half, zero, xp)], axis=0)
                y_pairs.append(_dot(lhs, rhs))
            y = (jnp.concatenate(y_pairs, axis=1) + y_off
                 + xcv_ref[:, gcols] * dskip_ref[:, gcols])
            zf = z_ref[rows, gcols].astype(F32)
            yz = y * (zf * _sigmoid(zf))
            ms = jnp.mean(yz * yz, axis=-1, keepdims=True)
            out_ref[rows, gcols] = (yz * lax.rsqrt(ms + EPS) * nw_ref[:, gcols]).astype(BF16)
        return carry

    lax.fori_loop(0, TS_SSD // L, chunk, 0, unroll=True)


def _ssd(fused, dt, conv_w, conv_b, a_log, d_skip, norm_w, bsz, seq, layer):
    ntok = fused.shape[0]
    nt = seq // TS_SSD
    row = lambda b, s: (b * nt + s, 0)
    return pl.pallas_call(
        _ssd_kernel,
        grid=(bsz, nt),
        in_specs=[
            pl.BlockSpec((TS_SSD, SSD_INNER), lambda b, s: (b * nt + s, Z_BLOCK)),
            pl.BlockSpec((TS_SSD, SSD_CONV_DIM), lambda b, s: (b * nt + s, XBC_BLOCK)),
            pl.BlockSpec((TS_SSD, LANES), row),
            _const_spec((SSD_CONV, SSD_CONV_DIM), layer),
            _const_spec((1, SSD_CONV_DIM), layer),
            _const_spec((1, LANES), layer),
            _const_spec((1, SSD_INNER), layer),
            _const_spec((1, SSD_INNER), layer),
        ],
        out_specs=pl.BlockSpec((TS_SSD, SSD_INNER), row),
        out_shape=jax.ShapeDtypeStruct((ntok, SSD_INNER), BF16),
        scratch_shapes=[
            pltpu.VMEM((SSD_GROUPS, SSD_STATE, GROUP_WIDTH), F32),
            pltpu.VMEM((BF16_ROWS, SSD_CONV_DIM), BF16),
            pltpu.VMEM((SSD_CONV * BF16_ROWS, SSD_CONV_DIM), BF16),
            pltpu.VMEM((SSD_CHUNK, SSD_CONV_DIM), F32),
            pltpu.VMEM((SSD_CHUNK, SSD_INNER), BF16),
            pltpu.VMEM((SSD_CHUNK, SSD_INNER), BF16),
            pltpu.VMEM((SSD_CHUNK, SSD_INNER), F32),
        ],
        compiler_params=pltpu.CompilerParams(
            dimension_semantics=("arbitrary", "arbitrary"), vmem_limit_bytes=VMEM_LIMIT),
        name="ssd",
    )(fused, fused, dt, conv_w, conv_b, a_log, d_skip, norm_w)


def _merge_kernel(yt_ref, ya_ref, gs_ref, bg_ref, x_ref, cb_ref, lng_ref, lnb_ref,
                  wa_ref, wb_ref, wo_ref, out_ref):
    c = jnp.concatenate([yt_ref[:, j, :].T for j in range(TM_MERGE // CONV_BLOCK)], axis=0)
    c = c + cb_ref[...]
    mu = jnp.mean(c, axis=-1, keepdims=True)
    d = c - mu
    var = jnp.mean(d * d, axis=-1, keepdims=True)
    yb = d * lax.rsqrt(var + EPS) * lng_ref[...] + lnb_ref[...]
    yb = (yb * _sigmoid(yb)).astype(BF16)
    pa = _dot(ya_ref[...], wa_ref[...])
    pb = _dot(yb, wb_ref[...])
    ga = _sigmoid(gs_ref[:, :D_MODEL].astype(F32) + bg_ref[:, :D_MODEL])
    gb = _sigmoid(gs_ref[:, D_MODEL:].astype(F32) + bg_ref[:, D_MODEL:])
    merged = ga * pa + gb * pb
    out_ref[...] = x_ref[...] + _dot(merged.astype(BF16), wo_ref[...])


def _merge(yt, ya, fused, b_gate, x2d, conv_b, ln_g, ln_b, w_a, w_b, w_o, layer):
    ntok = ya.shape[0]
    row = lambda i: (i, 0)
    return pl.pallas_call(
        _merge_kernel,
        grid=(ntok // TM_MERGE,),
        in_specs=[
            pl.BlockSpec((CM_DIM, TM_MERGE // CONV_BLOCK, CONV_BLOCK), lambda i: (0, i, 0)),
            pl.BlockSpec((TM_MERGE, SSD_INNER), row),
            pl.BlockSpec((TM_MERGE, 2 * D_MODEL), lambda i: (i, GATES_BLOCK)),
            _const_spec((1, 2 * D_MODEL), layer),
            pl.BlockSpec((TM_MERGE, D_MODEL), row),
            _const_spec((1, CM_DIM), layer),
            _const_spec((1, CM_DIM), layer),
            _const_spec((1, CM_DIM), layer),
            _const_spec((SSD_INNER, D_MODEL), layer),
            _const_spec((CM_DIM, D_MODEL), layer),
            _const_spec((D_MODEL, D_MODEL), layer),
        ],
        out_specs=pl.BlockSpec((TM_MERGE, D_MODEL), row),
        out_shape=jax.ShapeDtypeStruct((ntok, D_MODEL), F32),
        compiler_params=pltpu.CompilerParams(
            dimension_semantics=("arbitrary",), vmem_limit_bytes=VMEM_LIMIT),
        name="merge",
    )(yt, ya, fused, b_gate, x2d, conv_b, ln_g, ln_b, w_a, w_b, w_o)


def _ffn_kernel(x_ref, p_ref, gf_ref, wup_ref, fcw_ref, fcb_ref, wdn_ref, gp_ref, wpg_ref,
                wpe_ref, gfin_ref, out_ref, h_ref, tail_ref, acc_ref, *, final):
    @pl.when(pl.program_id(1) == 0)
    def _():
        tail_ref[...] = jnp.zeros_like(tail_ref)

    xf = x_ref[...]
    h_ref[...] = _rms(xf, gf_ref[...]).astype(BF16)
    acc_ref[...] = xf

    def up_cols(c, half):
        return slice(half * FFN_DIM + c * CW_FFN, half * FFN_DIM + (c + 1) * CW_FFN)

    def up(c):
        return [_dot(h_ref[...], wup_ref[:, up_cols(c, half)]) for half in range(2)]

    n_chunks = FFN_DIM // CW_FFN
    pending = [up(c) for c in range(FFN_LOOKAHEAD)]
    for c in range(n_chunks):
        if c + FFN_LOOKAHEAD < n_chunks:
            pending.append(up(c + FFN_LOOKAHEAD))
        current = pending.pop(0)
        halves = []
        for half, u in enumerate(current):
            cols = up_cols(c, half)
            ext = jnp.concatenate([tail_ref[:, cols], u], axis=0)
            y = fcb_ref[:, cols] + fcw_ref[FFN_CONV - 1:FFN_CONV, cols] * u
            for sft in range(1, FFN_CONV):
                k = FFN_CONV - 1 - sft
                y = y + fcw_ref[k:k + 1, cols] * _shift_rows(ext, sft)
            tail_ref[:, cols] = u[TS_FFN - SUBLANES:, :]
            halves.append(y)
        gate, val = halves
        act = (gate * _sigmoid(gate) * val).astype(BF16)
        acc_ref[...] += _dot(act, wdn_ref[c * CW_FFN:(c + 1) * CW_FFN, :])
    x1 = acc_ref[...]
    h2 = _rms(x1, gp_ref[...]).astype(BF16)
    pg = _sigmoid(_dot(h2, wpg_ref[...]))
    pe = _dot(p_ref[...].astype(BF16), wpe_ref[...])
    x2 = x1 + pg * pe
    if final:
        x2 = _rms(x2, gfin_ref[...])
    out_ref[...] = x2


def _ffn(x2d, p3d, g_ffn, w_up, conv_w, conv_b, w_down, g_ple, w_pg, w_pe, g_final,
         bsz, seq, layer, final):
    ntok = x2d.shape[0]
    nt = seq // TS_FFN
    row = lambda b, s: (b * nt + s, 0)
    return pl.pallas_call(
        functools.partial(_ffn_kernel, final=final),
        grid=(bsz, nt),
        in_specs=[
            pl.BlockSpec((TS_FFN, D_MODEL), row),
            pl.BlockSpec((None, TS_FFN, PLE_DIM), lambda b, s: (layer, b * nt + s, 0)),
            _const_spec((1, D_MODEL), layer),
            _const_spec((D_MODEL, 2 * FFN_DIM), layer),
            _const_spec((FFN_CONV, 2 * FFN_DIM), layer),
            _const_spec((1, 2 * FFN_DIM), layer),
            _const_spec((FFN_DIM, D_MODEL), layer),
            _const_spec((1, D_MODEL), layer),
            _const_spec((D_MODEL, D_MODEL), layer),
            _const_spec((PLE_DIM, D_MODEL), layer),
            _const_spec((1, D_MODEL)),
        ],
        out_specs=pl.BlockSpec((TS_FFN, D_MODEL), row),
        out_shape=jax.ShapeDtypeStruct((ntok, D_MODEL), F32),
        scratch_shapes=[
            pltpu.VMEM((TS_FFN, D_MODEL), BF16),
            pltpu.VMEM((SUBLANES, 2 * FFN_DIM), F32),
            pltpu.VMEM((TS_FFN, D_MODEL), F32),
        ],
        compiler_params=pltpu.CompilerParams(
            dimension_semantics=("arbitrary", "arbitrary"), vmem_limit_bytes=VMEM_LIMIT),
        name="ffn_ple",
    )(x2d, p3d, g_ffn, w_up, conv_w, conv_b, w_down, g_ple, w_pg, w_pe, g_final)


def _prep_w_in_kernel(w_ref, wall_ref, wdt_ref):
    dt0 = SSD_INNER + SSD_CONV_DIM
    val0 = dt0 + SSD_HEADS
    gs0 = val0 + 2 * CM_DIM
    w = w_ref[...]
    pieces = ((val0, gs0), (SSD_INNER, dt0), (gs0, gs0 + 2 * D_MODEL), (0, SSD_INNER))
    at = 0
    for lo, hi in pieces:
        wall_ref[:, at:at + hi - lo] = w[:, lo:hi].astype(BF16)
        at += hi - lo
    lane = lax.broadcasted_iota(jnp.int32, (w.shape[0], LANES), 1)
    wdt_ref[...] = jnp.where(lane < SSD_HEADS, w[:, dt0:dt0 + LANES], 0.0).astype(BF16)


def _prep_w_in(w_in):
    nl, d, ncol = w_in.shape
    spec = lambda width: pl.BlockSpec((None, PREP_ROWS, width), lambda l, i: (l, i, 0))
    return pl.pallas_call(
        _prep_w_in_kernel,
        grid=(nl, d // PREP_ROWS),
        in_specs=[spec(ncol)],
        out_specs=[spec(2 * CM_DIM + FUSED_DIM), spec(LANES)],
        out_shape=[jax.ShapeDtypeStruct((nl, d, 2 * CM_DIM + FUSED_DIM), BF16),
                   jax.ShapeDtypeStruct((nl, d, LANES), BF16)],
        compiler_params=pltpu.CompilerParams(
            dimension_semantics=("arbitrary", "arbitrary"), vmem_limit_bytes=VMEM_LIMIT),
        name="prep_w_in",
    )(w_in)


def _pad_lanes(a, width):
    return jnp.pad(a, [(0, 0)] * (a.ndim - 1) + [(0, width - a.shape[-1])])


def _rows(a):
    return a[:, None, :]


def kernel(x, p, norm_mix, w_in, b_gate, ssd_conv_w, ssd_conv_b, dt_bias, a_log, d_skip, ssd_norm, w_ssd_out, cm_conv_w, cm_conv_b, cm_ln_g, cm_ln_b, w_cm_out, w_out, norm_ffn, w_up, ffn_conv_w, ffn_conv_b, w_down, norm_ple, w_ple_gate, w_ple_proj, final_norm):
    bsz, seq, d = x.shape
    depth = w_in.shape[0]
    ntok = bsz * seq
    w_all, w_dt = _prep_w_in(w_in)
    tw = _toeplitz(_pad_lanes(jnp.swapaxes(cm_conv_w, 1, 2), LANES).astype(BF16))
    w_ssd_out_b, w_cm_out_b, w_out_b = _to_bf16(w_ssd_out), _to_bf16(w_cm_out), _to_bf16(w_out)
    w_up_b, w_down_b = _to_bf16(w_up), _to_bf16(w_down)
    w_pg_b, w_pe_b = _to_bf16(w_ple_gate), _to_bf16(w_ple_proj)
    norm_mix_r, dt_bias_r = _rows(norm_mix), _rows(_pad_lanes(dt_bias, LANES))
    ssd_conv_b_r, a_log_r = _rows(ssd_conv_b), _rows(_pad_lanes(a_log, LANES))
    d_skip_r, ssd_norm_r = _rows(jnp.repeat(d_skip, SSD_HEAD_DIM, axis=1)), _rows(ssd_norm)
    b_gate_r = b_gate.reshape(depth, 1, -1)
    cm_conv_b_r, cm_ln_g_r, cm_ln_b_r = _rows(cm_conv_b), _rows(cm_ln_g), _rows(cm_ln_b)
    norm_ffn_r, ffn_conv_b_r, norm_ple_r = _rows(norm_ffn), _rows(ffn_conv_b), _rows(norm_ple)
    p3d = p.reshape(depth, ntok, PLE_DIM)

    x2d = x.reshape(ntok, d)
    for i in range(depth):
        fused, vt, dt = _inproj(x2d, norm_mix_r, w_all, w_dt, dt_bias_r, i)
        yt = _conv31(vt, tw, seq, i)
        ya = _ssd(fused, dt, ssd_conv_w, ssd_conv_b_r, a_log_r, d_skip_r, ssd_norm_r, bsz, seq, i)
        x2d = _merge(yt, ya, fused, b_gate_r, x2d, cm_conv_b_r, cm_ln_g_r, cm_ln_b_r,
                     w_ssd_out_b, w_cm_out_b, w_out_b, i)
        x2d = _ffn(x2d, p3d, norm_ffn_r, w_up_b, ffn_conv_w, ffn_conv_b_r, w_down_b, norm_ple_r,
                   w_pg_b, w_pe_b, final_norm[None], bsz, seq, i, final=(i == depth - 1))
    return x2d.reshape(bsz, seq, d)
```

```python
import functools

import jax
import jax.numpy as jnp
from jax import lax
from jax.experimental import pallas as pl
from jax.experimental.pallas import tpu as pltpu

F32 = jnp.float32
BF16 = jnp.bfloat16

D_MODEL = 1024
PLE_DIM = 256
SSD_HEADS = 16
SSD_HEAD_DIM = 64
SSD_INNER = SSD_HEADS * SSD_HEAD_DIM
SSD_GROUPS = 4
SSD_STATE = 128
SSD_CONV = 4
SSD_CHUNK = 128
SSD_CONV_DIM = SSD_INNER + 2 * SSD_GROUPS * SSD_STATE
GROUP_WIDTH = SSD_INNER // SSD_GROUPS
CM_DIM = D_MODEL
CM_KERNEL = 31
FFN_DIM = 2816
FFN_CONV = 3
EPS = 1e-6

LANES = 128
SUBLANES = 8
BF16_ROWS = 16
VMEM_LIMIT = 56 * 1024 * 1024

TM_IN = 512
NC_IN = 512
FUSED_DIM = SSD_CONV_DIM + 2 * D_MODEL + SSD_INNER
XBC_BLOCK = 0
GATES_BLOCK = SSD_CONV_DIM // (2 * D_MODEL)
Z_BLOCK = (SSD_CONV_DIM + 2 * D_MODEL) // SSD_INNER
TS_SSD = 512
TM_MERGE = 1024
CONV_BLOCK = 128
CONV_CHANNELS = 32
CONV_UNROLL = 8
TS_FFN = 256
CW_FFN = 256
CAST_BLOCK_BYTES = 12 * 1024 * 1024
PREP_ROWS = 128
FFN_LOOKAHEAD = 3


def _sigmoid(x):
    return jax.nn.sigmoid(x)


def _rms(xf, g):
    ms = jnp.mean(xf * xf, axis=-1, keepdims=True)
    return xf * lax.rsqrt(ms + EPS) * g


def _dot(a, b):
    return jnp.dot(a, b, preferred_element_type=F32)


def _split_bf16(x):
    hi = x.astype(BF16)
    lo = (x - hi.astype(F32)).astype(BF16)
    return hi, lo


def _shift_rows(ext, shift):
    return pltpu.roll(ext, shift, 0)[SUBLANES:, :]


def _const_spec(shape, layer=None):
    nd = len(shape)
    if layer is None:
        return pl.BlockSpec(shape, lambda *_: (0,) * nd, pipeline_mode=pl.Buffered(1))
    return pl.BlockSpec((None,) + shape, lambda *_: (layer,) + (0,) * nd,
                        pipeline_mode=pl.Buffered(1))


def _cast_kernel(w_ref, o_ref):
    o_ref[...] = w_ref[...].astype(BF16)


def _to_bf16(w):
    nl, r, c = w.shape
    rb = r
    while rb * c * w.dtype.itemsize > CAST_BLOCK_BYTES and rb % 2 == 0:
        rb //= 2
    spec = pl.BlockSpec((None, rb, c), lambda l, i: (l, i, 0))
    return pl.pallas_call(
        _cast_kernel,
        grid=(nl, r // rb),
        in_specs=[spec],
        out_specs=spec,
        out_shape=jax.ShapeDtypeStruct(w.shape, BF16),
        compiler_params=pltpu.CompilerParams(
            dimension_semantics=("arbitrary", "arbitrary"), vmem_limit_bytes=VMEM_LIMIT),
        name="cast_bf16",
    )(w)


def _inproj_kernel(x_ref, g_ref, w_ref, wdt_ref, dtb_ref, fused_ref, vt_ref, dt_ref, h_ref):
    h_ref[...] = _rms(x_ref[...], g_ref[...]).astype(BF16)

    def mm(c0):
        return _dot(h_ref[...], w_ref[:, c0:c0 + NC_IN])

    for j in range(0, CM_DIM, NC_IN):
        v = mm(j) * _sigmoid(mm(CM_DIM + j))
        vt_ref[j:j + NC_IN, :] = v.T.astype(BF16)
    for j in range(0, FUSED_DIM, NC_IN):
        fused_ref[:, j:j + NC_IN] = mm(2 * CM_DIM + j).astype(BF16)
    dtr = _dot(h_ref[...], wdt_ref[...]) + dtb_ref[...]
    dt = jnp.maximum(dtr, 0.0) + jnp.log1p(jnp.exp(-jnp.abs(dtr)))
    lane = lax.broadcasted_iota(jnp.int32, dt.shape, 1)
    dt_ref[...] = jnp.where(lane < SSD_HEADS, dt, 0.0)


def _inproj(x2d, g, w_all, w_dt, dt_bias, layer):
    ntok = x2d.shape[0]
    row = lambda i: (i, 0)
    return pl.pallas_call(
        _inproj_kernel,
        grid=(ntok // TM_IN,),
        in_specs=[
            pl.BlockSpec((TM_IN, D_MODEL), row),
            _const_spec((1, D_MODEL), layer),
            _const_spec((D_MODEL, 2 * CM_DIM + FUSED_DIM), layer),
            _const_spec((D_MODEL, LANES), layer),
            _const_spec((1, LANES), layer),
        ],
        out_specs=[
            pl.BlockSpec((TM_IN, FUSED_DIM), row),
            pl.BlockSpec((CM_DIM, TM_IN), lambda i: (0, i)),
            pl.BlockSpec((TM_IN, LANES), row),
        ],
        out_shape=[
            jax.ShapeDtypeStruct((ntok, FUSED_DIM), BF16),
            jax.ShapeDtypeStruct((CM_DIM, ntok), BF16),
            jax.ShapeDtypeStruct((ntok, LANES), F32),
        ],
        scratch_shapes=[pltpu.VMEM((TM_IN, D_MODEL), BF16)],
        compiler_params=pltpu.CompilerParams(
            dimension_semantics=("arbitrary",), vmem_limit_bytes=VMEM_LIMIT),
        name="inproj",
    )(x2d, g, w_all, w_dt, dt_bias)


def _toeplitz_kernel(wt_ref, out_ref):
    ncol = out_ref.shape[1]
    q = pl.program_id(1) * ncol + lax.broadcasted_iota(jnp.int32, (LANES, ncol), 1)
    k = lax.broadcasted_iota(jnp.int32, (LANES, ncol), 0)
    s = jnp.right_shift(q, 8)
    n = jnp.bitwise_and(q, 2 * CONV_BLOCK - 1)
    lag = jnp.bitwise_and(n, CONV_BLOCK - 1) - s + jnp.bitwise_and(n, CONV_BLOCK)
    onehot = (k == (CM_KERNEL - 1) - lag).astype(BF16)
    out_ref[...] = _dot(wt_ref[...], onehot).astype(BF16)


def _toeplitz(wt):
    nl = wt.shape[0]
    ncol = 2048
    total = CONV_BLOCK * 2 * CONV_BLOCK
    tw = pl.pallas_call(
        _toeplitz_kernel,
        grid=(nl, total // ncol),
        in_specs=[pl.BlockSpec((None, CM_DIM, LANES), lambda l, i: (l, 0, 0))],
        out_specs=pl.BlockSpec((None, CM_DIM, ncol), lambda l, i: (l, 0, i)),
        out_shape=jax.ShapeDtypeStruct((nl, CM_DIM, total), BF16),
        compiler_params=pltpu.CompilerParams(
            dimension_semantics=("arbitrary", "arbitrary"), vmem_limit_bytes=VMEM_LIMIT),
        name="toeplitz",
    )(wt)
    return tw.reshape(nl, CM_DIM, CONV_BLOCK, 2 * CONV_BLOCK)


def _conv31_kernel(vt_ref, tw_ref, yt_ref, *, blocks_per_seq):
    nb = vt_ref.shape[1]
    blk = lax.broadcasted_iota(jnp.int32, (nb, CONV_BLOCK), 0)
    has_prev = jnp.bitwise_and(blk, blocks_per_seq - 1) != 0

    def channels(i, carry):
        for u in range(CONV_UNROLL):
            c = i * CONV_UNROLL + u
            z = _dot(vt_ref[c], tw_ref[c])
            from_prev = pltpu.roll(z[:, CONV_BLOCK:], 1, 0)
            yt_ref[c] = z[:, :CONV_BLOCK] + jnp.where(has_prev, from_prev, 0.0)
        return carry

    lax.fori_loop(0, vt_ref.shape[0] // CONV_UNROLL, channels, 0)


def _conv31(vt, tw, seq, layer):
    nch, ntok = vt.shape
    nb = ntok // CONV_BLOCK
    blocks_per_seq = seq // CONV_BLOCK
    assert blocks_per_seq & (blocks_per_seq - 1) == 0
    spec = pl.BlockSpec((CONV_CHANNELS, nb, CONV_BLOCK), lambda i: (i, 0, 0))
    return pl.pallas_call(
        functools.partial(_conv31_kernel, blocks_per_seq=blocks_per_seq),
        grid=(nch // CONV_CHANNELS,),
        in_specs=[spec, pl.BlockSpec((None, CONV_CHANNELS, CONV_BLOCK, 2 * CONV_BLOCK),
                                     lambda i: (layer, i, 0, 0))],
        out_specs=spec,
        out_shape=jax.ShapeDtypeStruct((nch, nb, CONV_BLOCK), F32),
        compiler_params=pltpu.CompilerParams(
            dimension_semantics=("arbitrary",), vmem_limit_bytes=VMEM_LIMIT),
        name="conv31",
    )(vt.reshape(nch, nb, CONV_BLOCK), tw)


def _ssd_kernel(z_ref, xbc_ref, dt_ref, cw_ref, cb_ref, alog_ref, dskip_ref, nw_ref,
                out_ref, state_ref, tail_ref, cwb_ref, xcv_ref, xdt_ref, xw_ref, eac_ref):
    L = SSD_CHUNK

    @pl.when(pl.program_id(1) == 0)
    def _():
        state_ref[...] = jnp.zeros_like(state_ref)
        tail_ref[...] = jnp.zeros_like(tail_ref)

    for k in range(SSD_CONV):
        cwb_ref[k * BF16_ROWS:(k + 1) * BF16_ROWS, :] = jnp.broadcast_to(
            cw_ref[k:k + 1, :], (BF16_ROWS, SSD_CONV_DIM)).astype(BF16)

    li = lax.broadcasted_iota(jnp.int32, (L, L), 0)
    si = lax.broadcasted_iota(jnp.int32, (L, L), 1)
    causal = si <= li
    tri = causal.astype(BF16)
    hrow = lax.broadcasted_iota(jnp.int32, (LANES, SSD_INNER), 0)
    hcol = lax.broadcasted_iota(jnp.int32, (LANES, SSD_INNER), 1)
    head_of = jnp.right_shift(hcol, SSD_HEAD_DIM.bit_length() - 1)
    expand = (head_of == hrow).astype(BF16)
    expand2 = jnp.concatenate([expand, expand], axis=0)
    tri2 = jnp.concatenate([tri, tri], axis=1)
    low_half = si < SSD_HEAD_DIM
    ext_rows = BF16_ROWS + L
    shift_k = -(-SSD_CONV * ext_rows // LANES) * LANES
    st = lax.broadcasted_iota(jnp.int32, (L, shift_k), 0)
    sc = lax.broadcasted_iota(jnp.int32, (L, shift_k), 1)
    hit = sc < 0
    for k in range(SSD_CONV):
        hit = hit | (sc == st + (k * ext_rows + BF16_ROWS - (SSD_CONV - 1 - k)))
    shift_mat = hit.astype(BF16)
    a_row = -jnp.exp(alog_ref[...])

    def chunk(ci, carry):
        rows = pl.ds(pl.multiple_of(ci * L, L), L)

        cblk = 512
        for c0 in range(0, SSD_CONV_DIM, cblk):
            cols = slice(c0, c0 + cblk)
            ext = jnp.concatenate([tail_ref[:, cols], xbc_ref[rows, cols]], axis=0)
            ext = ext.reshape(ext_rows // BF16_ROWS, BF16_ROWS, cblk)
            prods = [(ext * cwb_ref[k * BF16_ROWS:(k + 1) * BF16_ROWS, cols][None]
                      ).reshape(ext_rows, cblk) for k in range(SSD_CONV)]
            prods.append(jnp.zeros((shift_k - SSD_CONV * ext_rows, cblk), BF16))
            acc = _dot(shift_mat, jnp.concatenate(prods, axis=0)) + cb_ref[:, cols]
            tail_ref[:, cols] = xbc_ref[pl.ds(pl.multiple_of(ci * L + L - BF16_ROWS, BF16_ROWS),
                                              BF16_ROWS), cols]
            xcv_ref[:, cols] = acc * _sigmoid(acc)

        dtc = dt_ref[rows, :]
        acum = _dot(tri2, jnp.concatenate(_split_bf16(dtc * a_row), axis=0))
        acum_t = acum.T
        both = jnp.concatenate(_split_bf16(jnp.concatenate([dtc, acum], axis=0)), axis=1)
        ex = _dot(both, expand2)
        dt_exp = ex[:L]
        ac_exp = ex[L:]
        a_last = ac_exp[L - 1:L, :]
        xdt = xcv_ref[:, :SSD_INNER] * dt_exp
        xdt_ref[...] = xdt.astype(BF16)
        xw_ref[...] = (xdt * jnp.exp(a_last - ac_exp)).astype(BF16)
        eac_ref[...] = jnp.exp(ac_exp)
        chunk_decay = jnp.exp(a_last)

        for g in range(SSD_GROUPS):
            gcols = slice(g * GROUP_WIDTH, (g + 1) * GROUP_WIDTH)
            b0 = SSD_INNER + g * SSD_STATE
            c0 = SSD_INNER + SSD_GROUPS * SSD_STATE + g * SSD_STATE
            bm_f = xcv_ref[:, b0:b0 + SSD_STATE]
            bm_b = bm_f.astype(BF16)
            bmt_b = bm_f.T.astype(BF16)
            cm_b = xcv_ref[:, c0:c0 + SSD_STATE].astype(BF16)
            cbm = lax.dot_general(cm_b, bm_b, (((1,), (1,)), ((), ())),
                                  preferred_element_type=F32)
            st = state_ref[g]
            y_off = _dot(cm_b, st.astype(BF16)) * eac_ref[:, gcols]
            state_ref[g] = st * chunk_decay[:, gcols] + _dot(bmt_b, xw_ref[:, gcols])

            y_pairs = []
            for jj in range(2):
                j = 2 * g + jj
                scores = []
                for h in (2 * j, 2 * j + 1):
                    seg = acum[:, h:h + 1] - acum_t[h:h + 1, :]
                    decay = jnp.exp(jnp.where(causal, seg, -jnp.inf))
                    scores.append((cbm * decay).astype(BF16))
                lhs = jnp.concatenate(scores, axis=1)
                xp = xdt_ref[:, j * LANES:(j + 1) * LANES]
                zero = jnp.zeros_like(xp)
                rhs = jnp.concatenate([jnp.where(low_half, xp, zero),
                                       jnp.where(low_half, zero, xp)], axis=0)
                y_pairs.append(_dot(lhs, rhs))
            y = (jnp.concatenate(y_pairs, axis=1) + y_off
                 + xcv_ref[:, gcols] * dskip_ref[:, gcols])
            zf = z_ref[rows, gcols].astype(F32)
            yz = y * (zf * _sigmoid(zf))
            ms = jnp.mean(yz * yz, axis=-1, keepdims=True)
            out_ref[rows, gcols] = (yz * lax.rsqrt(ms + EPS) * nw_ref[:, gcols]).astype(BF16)
        return carry

    lax.fori_loop(0, TS_SSD // L, chunk, 0, unroll=True)


def _ssd(fused, dt, conv_w, conv_b, a_log, d_skip, norm_w, bsz, seq, layer):
    ntok = fused.shape[0]
    nt = seq // TS_SSD
    row = lambda b, s: (b * nt + s, 0)
    return pl.pallas_call(
        _ssd_kernel,
        grid=(bsz, nt),
        in_specs=[
            pl.BlockSpec((TS_SSD, SSD_INNER), lambda b, s: (b * nt + s, Z_BLOCK)),
            pl.BlockSpec((TS_SSD, SSD_CONV_DIM), lambda b, s: (b * nt + s, XBC_BLOCK)),
            pl.BlockSpec((TS_SSD, LANES), row),
            _const_spec((SSD_CONV, SSD_CONV_DIM), layer),
            _const_spec((1, SSD_CONV_DIM), layer),
            _const_spec((1, LANES), layer),
            _const_spec((1, SSD_INNER), layer),
            _const_spec((1, SSD_INNER), layer),
        ],
        out_specs=pl.BlockSpec((TS_SSD, SSD_INNER), row),
        out_shape=jax.ShapeDtypeStruct((ntok, SSD_INNER), BF16),
        scratch_shapes=[
            pltpu.VMEM((SSD_GROUPS, SSD_STATE, GROUP_WIDTH), F32),
            pltpu.VMEM((BF16_ROWS, SSD_CONV_DIM), BF16),
            pltpu.VMEM((SSD_CONV * BF16_ROWS, SSD_CONV_DIM), BF16),
            pltpu.VMEM((SSD_CHUNK, SSD_CONV_DIM), F32),
            pltpu.VMEM((SSD_CHUNK, SSD_INNER), BF16),
            pltpu.VMEM((SSD_CHUNK, SSD_INNER), BF16),
            pltpu.VMEM((SSD_CHUNK, SSD_INNER), F32),
        ],
        compiler_params=pltpu.CompilerParams(
            dimension_semantics=("arbitrary", "arbitrary"), vmem_limit_bytes=VMEM_LIMIT),
        name="ssd",
    )(fused, fused, dt, conv_w, conv_b, a_log, d_skip, norm_w)


def _merge_kernel(yt_ref, ya_ref, gs_ref, bg_ref, x_ref, cb_ref, lng_ref, lnb_ref,
                  wa_ref, wb_ref, wo_ref, out_ref):
    c = jnp.concatenate([yt_ref[:, j, :].T for j in range(TM_MERGE // CONV_BLOCK)], axis=0)
    c = c + cb_ref[...]
    mu = jnp.mean(c, axis=-1, keepdims=True)
    d = c - mu
    var = jnp.mean(d * d, axis=-1, keepdims=True)
    yb = d * lax.rsqrt(var + EPS) * lng_ref[...] + lnb_ref[...]
    yb = (yb * _sigmoid(yb)).astype(BF16)
    pa = _dot(ya_ref[...], wa_ref[...])
    pb = _dot(yb, wb_ref[...])
    ga = _sigmoid(gs_ref[:, :D_MODEL].astype(F32) + bg_ref[:, :D_MODEL])
    gb = _sigmoid(gs_ref[:, D_MODEL:].astype(F32) + bg_ref[:, D_MODEL:])
    merged = ga * pa + gb * pb
    out_ref[...] = x_ref[...] + _dot(merged.astype(BF16), wo_ref[...])


def _merge(yt, ya, fused, b_gate, x2d, conv_b, ln_g, ln_b, w_a, w_b, w_o, layer):
    ntok = ya.shape[0]
    row = lambda i: (i, 0)
    return pl.pallas_call(
        _merge_kernel,
        grid=(ntok // TM_MERGE,),
        in_specs=[
            pl.BlockSpec((CM_DIM, TM_MERGE // CONV_BLOCK, CONV_BLOCK), lambda i: (0, i, 0)),
            pl.BlockSpec((TM_MERGE, SSD_INNER), row),
            pl.BlockSpec((TM_MERGE, 2 * D_MODEL), lambda i: (i, GATES_BLOCK)),
            _const_spec((1, 2 * D_MODEL), layer),
            pl.BlockSpec((TM_MERGE, D_MODEL), row),
            _const_spec((1, CM_DIM), layer),
            _const_spec((1, CM_DIM), layer),
            _const_spec((1, CM_DIM), layer),
            _const_spec((SSD_INNER, D_MODEL), layer),
            _const_spec((CM_DIM, D_MODEL), layer),
            _const_spec((D_MODEL, D_MODEL), layer),
        ],
        out_specs=pl.BlockSpec((TM_MERGE, D_MODEL), row),
        out_shape=jax.ShapeDtypeStruct((ntok, D_MODEL), F32),
        compiler_params=pltpu.CompilerParams(
            dimension_semantics=("arbitrary",), vmem_limit_bytes=VMEM_LIMIT),
        name="merge",
    )(yt, ya, fused, b_gate, x2d, conv_b, ln_g, ln_b, w_a, w_b, w_o)


def _ffn_kernel(x_ref, p_ref, gf_ref, wup_ref, fcw_ref, fcb_ref, wdn_ref, gp_ref, wpg_ref,
                wpe_ref, gfin_ref, out_ref, h_ref, tail_ref, acc_ref, *, final):
    @pl.when(pl.program_id(1) == 0)
    def _():
        tail_ref[...] = jnp.zeros_like(tail_ref)

    xf = x_ref[...]
    h_ref[...] = _rms(xf, gf_ref[...]).astype(BF16)
    acc_ref[...] = xf

    def up_cols(c, half):
        return slice(half * FFN_DIM + c * CW_FFN, half * FFN_DIM + (c + 1) * CW_FFN)

    def up(c):
        return [_dot(h_ref[...], wup_ref[:, up_cols(c, half)]) for half in range(2)]

    n_chunks = FFN_DIM // CW_FFN
    pending = [up(c) for c in range(FFN_LOOKAHEAD)]
    for c in range(n_chunks):
        if c + FFN_LOOKAHEAD < n_chunks:
            pending.append(up(c + FFN_LOOKAHEAD))
        current = pending.pop(0)
        halves = []
        for half, u in enumerate(current):
            cols = up_cols(c, half)
            ext = jnp.concatenate([tail_ref[:, cols], u], axis=0)
            y = fcb_ref[:, cols] + fcw_ref[FFN_CONV - 1:FFN_CONV, cols] * u
            for sft in range(1, FFN_CONV):
                k = FFN_CONV - 1 - sft
                y = y + fcw_ref[k:k + 1, cols] * _shift_rows(ext, sft)
            tail_ref[:, cols] = u[TS_FFN - SUBLANES:, :]
            halves.append(y)
        gate, val = halves
        act = (gate * _sigmoid(gate) * val).astype(BF16)
        acc_ref[...] += _dot(act, wdn_ref[c * CW_FFN:(c + 1) * CW_FFN, :])
    x1 = acc_ref[...]
    h2 = _rms(x1, gp_ref[...]).astype(BF16)
    pg = _sigmoid(_dot(h2, wpg_ref[...]))
    pe = _dot(p_ref[...].astype(BF16), wpe_ref[...])
    x2 = x1 + pg * pe
    if final:
        x2 = _rms(x2, gfin_ref[...])
    out_ref[...] = x2


def _ffn(x2d, p3d, g_ffn, w_up, conv_w, conv_b, w_down, g_ple, w_pg, w_pe, g_final,
         bsz, seq, layer, final):
    ntok = x2d.shape[0]
    nt = seq // TS_FFN
    row = lambda b, s: (b * nt + s, 0)
    return pl.pallas_call(
        functools.partial(_ffn_kernel, final=final),
        grid=(bsz, nt),
        in_specs=[
            pl.BlockSpec((TS_FFN, D_MODEL), row),
            pl.BlockSpec((None, TS_FFN, PLE_DIM), lambda b, s: (layer, b * nt + s, 0)),
            _const_spec((1, D_MODEL), layer),
            _const_spec((D_MODEL, 2 * FFN_DIM), layer),
            _const_spec((FFN_CONV, 2 * FFN_DIM), layer),
            _const_spec((1, 2 * FFN_DIM), layer),
            _const_spec((FFN_DIM, D_MODEL), layer),
            _const_spec((1, D_MODEL), layer),
            _const_spec((D_MODEL, D_MODEL), layer),
            _const_spec((PLE_DIM, D_MODEL), layer),
            _const_spec((1, D_MODEL)),
        ],
        out_specs=pl.BlockSpec((TS_FFN, D_MODEL), row),
        out_shape=jax.ShapeDtypeStruct((ntok, D_MODEL), F32),
        scratch_shapes=[
            pltpu.VMEM((TS_FFN, D_MODEL), BF16),
            pltpu.VMEM((SUBLANES, 2 * FFN_DIM), F32),
            pltpu.VMEM((TS_FFN, D_MODEL), F32),
        ],
        compiler_params=pltpu.CompilerParams(
            dimension_semantics=("arbitrary", "arbitrary"), vmem_limit_bytes=VMEM_LIMIT),
        name="ffn_ple",
    )(x2d, p3d, g_ffn, w_up, conv_w, conv_b, w_down, g_ple, w_pg, w_pe, g_final)


def _prep_w_in_kernel(wt_ref, wall_ref, wdt_ref):
    dt0 = SSD_INNER + SSD_CONV_DIM
    val0 = dt0 + SSD_HEADS
    gs0 = val0 + 2 * CM_DIM
    pieces = ((val0, gs0), (SSD_INNER, dt0), (gs0, gs0 + 2 * D_MODEL), (0, SSD_INNER))
    at = 0
    for lo, hi in pieces:
        wall_ref[:, at:at + hi - lo] = wt_ref[lo:hi, :].T.astype(BF16)
        at += hi - lo
    lane = lax.broadcasted_iota(jnp.int32, (PREP_ROWS, LANES), 1)
    wdt_ref[...] = jnp.where(lane < SSD_HEADS, wt_ref[dt0:dt0 + LANES, :].T, 0.0).astype(BF16)


def _prep_w_in(w_in):
    nl, d, ncol = w_in.shape
    out_spec = lambda width: pl.BlockSpec((None, PREP_ROWS, width), lambda l, i: (l, i, 0))
    return pl.pallas_call(
        _prep_w_in_kernel,
        grid=(nl, d // PREP_ROWS),
        in_specs=[pl.BlockSpec((None, ncol, PREP_ROWS), lambda l, i: (l, 0, i))],
        out_specs=[out_spec(2 * CM_DIM + FUSED_DIM), out_spec(LANES)],
        out_shape=[jax.ShapeDtypeStruct((nl, d, 2 * CM_DIM + FUSED_DIM), BF16),
                   jax.ShapeDtypeStruct((nl, d, LANES), BF16)],
        compiler_params=pltpu.CompilerParams(
            dimension_semantics=("arbitrary", "arbitrary"), vmem_limit_bytes=VMEM_LIMIT),
        name="prep_w_in",
    )(jnp.swapaxes(w_in, 1, 2))


def _pad_lanes(a, width):
    return jnp.pad(a, [(0, 0)] * (a.ndim - 1) + [(0, width - a.shape[-1])])


def _rows(a):
    return a[:, None, :]


def kernel(x, p, norm_mix, w_in, b_gate, ssd_conv_w, ssd_conv_b, dt_bias, a_log, d_skip, ssd_norm, w_ssd_out, cm_conv_w, cm_conv_b, cm_ln_g, cm_ln_b, w_cm_out, w_out, norm_ffn, w_up, ffn_conv_w, ffn_conv_b, w_down, norm_ple, w_ple_gate, w_ple_proj, final_norm):
    bsz, seq, d = x.shape
    depth = w_in.shape[0]
    ntok = bsz * seq
    w_all, w_dt = _prep_w_in(w_in)
    tw = _toeplitz(_pad_lanes(jnp.swapaxes(cm_conv_w, 1, 2), LANES).astype(BF16))
    w_ssd_out_b, w_cm_out_b, w_out_b = _to_bf16(w_ssd_out), _to_bf16(w_cm_out), _to_bf16(w_out)
    w_up_b, w_down_b = _to_bf16(w_up), _to_bf16(w_down)
    w_pg_b, w_pe_b = _to_bf16(w_ple_gate), _to_bf16(w_ple_proj)
    norm_mix_r, dt_bias_r = _rows(norm_mix), _rows(_pad_lanes(dt_bias, LANES))
    ssd_conv_b_r, a_log_r = _rows(ssd_conv_b), _rows(_pad_lanes(a_log, LANES))
    d_skip_r, ssd_norm_r = _rows(jnp.repeat(d_skip, SSD_HEAD_DIM, axis=1)), _rows(ssd_norm)
    b_gate_r = b_gate.reshape(depth, 1, -1)
    cm_conv_b_r, cm_ln_g_r, cm_ln_b_r = _rows(cm_conv_b), _rows(cm_ln_g), _rows(cm_ln_b)
    norm_ffn_r, ffn_conv_b_r, norm_ple_r = _rows(norm_ffn), _rows(ffn_conv_b), _rows(norm_ple)
    p3d = p.reshape(depth, ntok, PLE_DIM)

    x2d = x.reshape(ntok, d)
    for i in range(depth):
        fused, vt, dt = _inproj(x2d, norm_mix_r, w_all, w_dt, dt_bias_r, i)
        yt = _conv31(vt, tw, seq, i)
        ya = _ssd(fused, dt, ssd_conv_w, ssd_conv_b_r, a_log_r, d_skip_r, ssd_norm_r, bsz, seq, i)
        x2d = _merge(yt, ya, fused, b_gate_r, x2d, cm_conv_b_r, cm_ln_g_r, cm_ln_b_r,
                     w_ssd_out_b, w_cm_out_b, w_out_b, i)
        x2d = _ffn(x2d, p3d, norm_ffn_r, w_up_b, ffn_conv_w, ffn_conv_b_r, w_down_b, norm_ple_r,
                   w_pg_b, w_pe_b, final_norm[None], bsz, seq, i, final=(i == depth - 1))
    return x2d.reshape(bsz, seq, d)
```

```python
import functools

import jax
import jax.numpy as jnp
from jax import lax
from jax.experimental import pallas as pl
from jax.experimental.pallas import tpu as pltpu

F32 = jnp.float32
BF16 = jnp.bfloat16

D_MODEL = 1024
PLE_DIM = 256
SSD_HEADS = 16
SSD_HEAD_DIM = 64
SSD_INNER = SSD_HEADS * SSD_HEAD_DIM
SSD_GROUPS = 4
SSD_STATE = 128
SSD_CONV = 4
SSD_CHUNK = 128
SSD_CONV_DIM = SSD_INNER + 2 * SSD_GROUPS * SSD_STATE
GROUP_WIDTH = SSD_INNER // SSD_GROUPS
CM_DIM = D_MODEL
CM_KERNEL = 31
FFN_DIM = 2816
FFN_CONV = 3
EPS = 1e-6
LOG2_E = 1.4426950408889634

LANES = 128
SUBLANES = 8
BF16_ROWS = 16
VMEM_LIMIT = 56 * 1024 * 1024

TM_IN = 512
NC_IN = 512
FUSED_DIM = SSD_CONV_DIM + 2 * D_MODEL + SSD_INNER
XBC_BLOCK = 0
GATES_BLOCK = SSD_CONV_DIM // (2 * D_MODEL)
Z_BLOCK = (SSD_CONV_DIM + 2 * D_MODEL) // SSD_INNER
TS_SSD = 512
TM_MERGE = 1024
CONV_BLOCK = 128
CONV_CHANNELS = 32
CONV_UNROLL = 8
TS_FFN = 256
CW_FFN = 256
CAST_BLOCK_BYTES = 12 * 1024 * 1024
PREP_ROWS = 128
FFN_LOOKAHEAD = 3


def _sigmoid(x):
    return jax.nn.sigmoid(x)


def _rms(xf, g):
    ms = jnp.mean(xf * xf, axis=-1, keepdims=True)
    return xf * lax.rsqrt(ms + EPS) * g


def _dot(a, b):
    return jnp.dot(a, b, preferred_element_type=F32)


def _split_bf16(x):
    hi = x.astype(BF16)
    lo = (x - hi.astype(F32)).astype(BF16)
    return hi, lo


def _shift_rows(ext, shift):
    return pltpu.roll(ext, shift, 0)[SUBLANES:, :]


def _const_spec(shape, layer=None):
    nd = len(shape)
    if layer is None:
        return pl.BlockSpec(shape, lambda *_: (0,) * nd, pipeline_mode=pl.Buffered(1))
    return pl.BlockSpec((None,) + shape, lambda *_: (layer,) + (0,) * nd,
                        pipeline_mode=pl.Buffered(1))


def _cast_kernel(w_ref, o_ref):
    o_ref[...] = w_ref[...].astype(BF16)


def _to_bf16(w):
    nl, r, c = w.shape
    rb = r
    while rb * c * w.dtype.itemsize > CAST_BLOCK_BYTES and rb % 2 == 0:
        rb //= 2
    spec = pl.BlockSpec((None, rb, c), lambda l, i: (l, i, 0))
    return pl.pallas_call(
        _cast_kernel,
        grid=(nl, r // rb),
        in_specs=[spec],
        out_specs=spec,
        out_shape=jax.ShapeDtypeStruct(w.shape, BF16),
        compiler_params=pltpu.CompilerParams(
            dimension_semantics=("arbitrary", "arbitrary"), vmem_limit_bytes=VMEM_LIMIT),
        name="cast_bf16",
    )(w)


def _inproj_kernel(x_ref, g_ref, w_ref, wdt_ref, dtb_ref, fused_ref, vt_ref, dt_ref, h_ref):
    h_ref[...] = _rms(x_ref[...], g_ref[...]).astype(BF16)

    def mm(c0):
        return _dot(h_ref[...], w_ref[:, c0:c0 + NC_IN])

    for j in range(0, CM_DIM, NC_IN):
        v = mm(j) * _sigmoid(mm(CM_DIM + j))
        vt_ref[j:j + NC_IN, :] = v.T.astype(BF16)
    for j in range(0, FUSED_DIM, NC_IN):
        fused_ref[:, j:j + NC_IN] = mm(2 * CM_DIM + j).astype(BF16)
    dtr = _dot(h_ref[...], wdt_ref[...]) + dtb_ref[...]
    dt = jnp.maximum(dtr, 0.0) + jnp.log1p(jnp.exp(-jnp.abs(dtr)))
    lane = lax.broadcasted_iota(jnp.int32, dt.shape, 1)
    dt_ref[...] = jnp.where(lane < SSD_HEADS, dt, 0.0)


def _inproj(x2d, g, w_all, w_dt, dt_bias, layer):
    ntok = x2d.shape[0]
    row = lambda i: (i, 0)
    return pl.pallas_call(
        _inproj_kernel,
        grid=(ntok // TM_IN,),
        in_specs=[
            pl.BlockSpec((TM_IN, D_MODEL), row),
            _const_spec((1, D_MODEL), layer),
            _const_spec((D_MODEL, 2 * CM_DIM + FUSED_DIM), layer),
            _const_spec((D_MODEL, LANES), layer),
            _const_spec((1, LANES), layer),
        ],
        out_specs=[
            pl.BlockSpec((TM_IN, FUSED_DIM), row),
            pl.BlockSpec((CM_DIM, TM_IN), lambda i: (0, i)),
            pl.BlockSpec((TM_IN, LANES), row),
        ],
        out_shape=[
            jax.ShapeDtypeStruct((ntok, FUSED_DIM), BF16),
            jax.ShapeDtypeStruct((CM_DIM, ntok), BF16),
            jax.ShapeDtypeStruct((ntok, LANES), F32),
        ],
        scratch_shapes=[pltpu.VMEM((TM_IN, D_MODEL), BF16)],
        compiler_params=pltpu.CompilerParams(
            dimension_semantics=("arbitrary",), vmem_limit_bytes=VMEM_LIMIT),
        name="inproj",
    )(x2d, g, w_all, w_dt, dt_bias)


def _toeplitz_kernel(wt_ref, out_ref):
    ncol = out_ref.shape[1]
    q = pl.program_id(1) * ncol + lax.broadcasted_iota(jnp.int32, (LANES, ncol), 1)
    k = lax.broadcasted_iota(jnp.int32, (LANES, ncol), 0)
    s = jnp.right_shift(q, 8)
    n = jnp.bitwise_and(q, 2 * CONV_BLOCK - 1)
    lag = jnp.bitwise_and(n, CONV_BLOCK - 1) - s + jnp.bitwise_and(n, CONV_BLOCK)
    onehot = (k == (CM_KERNEL - 1) - lag).astype(BF16)
    out_ref[...] = _dot(wt_ref[...], onehot).astype(BF16)


def _toeplitz(wt):
    nl = wt.shape[0]
    ncol = 2048
    total = CONV_BLOCK * 2 * CONV_BLOCK
    tw = pl.pallas_call(
        _toeplitz_kernel,
        grid=(nl, total // ncol),
        in_specs=[pl.BlockSpec((None, CM_DIM, LANES), lambda l, i: (l, 0, 0))],
        out_specs=pl.BlockSpec((None, CM_DIM, ncol), lambda l, i: (l, 0, i)),
        out_shape=jax.ShapeDtypeStruct((nl, CM_DIM, total), BF16),
        compiler_params=pltpu.CompilerParams(
            dimension_semantics=("arbitrary", "arbitrary"), vmem_limit_bytes=VMEM_LIMIT),
        name="toeplitz",
    )(wt)
    return tw.reshape(nl, CM_DIM, CONV_BLOCK, 2 * CONV_BLOCK)


def _conv31_kernel(vt_ref, tw_ref, yt_ref, *, blocks_per_seq):
    nb = vt_ref.shape[1]
    blk = lax.broadcasted_iota(jnp.int32, (nb, CONV_BLOCK), 0)
    has_prev = jnp.bitwise_and(blk, blocks_per_seq - 1) != 0

    def channels(i, carry):
        for u in range(CONV_UNROLL):
            c = i * CONV_UNROLL + u
            z = _dot(vt_ref[c], tw_ref[c])
            from_prev = pltpu.roll(z[:, CONV_BLOCK:], 1, 0)
            yt_ref[c] = z[:, :CONV_BLOCK] + jnp.where(has_prev, from_prev, 0.0)
        return carry

    lax.fori_loop(0, vt_ref.shape[0] // CONV_UNROLL, channels, 0)


def _conv31(vt, tw, seq, layer):
    nch, ntok = vt.shape
    nb = ntok // CONV_BLOCK
    blocks_per_seq = seq // CONV_BLOCK
    assert blocks_per_seq & (blocks_per_seq - 1) == 0
    spec = pl.BlockSpec((CONV_CHANNELS, nb, CONV_BLOCK), lambda i: (i, 0, 0))
    return pl.pallas_call(
        functools.partial(_conv31_kernel, blocks_per_seq=blocks_per_seq),
        grid=(nch // CONV_CHANNELS,),
        in_specs=[spec, pl.BlockSpec((None, CONV_CHANNELS, CONV_BLOCK, 2 * CONV_BLOCK),
                                     lambda i: (layer, i, 0, 0))],
        out_specs=spec,
        out_shape=jax.ShapeDtypeStruct((nch, nb, CONV_BLOCK), F32),
        compiler_params=pltpu.CompilerParams(
            dimension_semantics=("arbitrary",), vmem_limit_bytes=VMEM_LIMIT),
        name="conv31",
    )(vt.reshape(nch, nb, CONV_BLOCK), tw)


def _ssd_kernel(z_ref, xbc_ref, dt_ref, cw_ref, cb_ref, alog_ref, dskip_ref, nw_ref,
                out_ref, state_ref, tail_ref, cwb_ref, xcv_ref, xdt_ref, xw_ref, eac_ref):
    L = SSD_CHUNK

    @pl.when(pl.program_id(1) == 0)
    def _():
        state_ref[...] = jnp.zeros_like(state_ref)
        tail_ref[...] = jnp.zeros_like(tail_ref)

    for k in range(SSD_CONV):
        cwb_ref[k * BF16_ROWS:(k + 1) * BF16_ROWS, :] = jnp.broadcast_to(
            cw_ref[k:k + 1, :], (BF16_ROWS, SSD_CONV_DIM)).astype(BF16)

    li = lax.broadcasted_iota(jnp.int32, (L, L), 0)
    si = lax.broadcasted_iota(jnp.int32, (L, L), 1)
    causal = si <= li
    tri = causal.astype(BF16)
    hrow = lax.broadcasted_iota(jnp.int32, (LANES, SSD_INNER), 0)
    hcol = lax.broadcasted_iota(jnp.int32, (LANES, SSD_INNER), 1)
    head_of = jnp.right_shift(hcol, SSD_HEAD_DIM.bit_length() - 1)
    expand = (head_of == hrow).astype(BF16)
    expand2 = jnp.concatenate([expand, expand], axis=0)
    tri2 = jnp.concatenate([tri, tri], axis=1)
    low_half = si < SSD_HEAD_DIM
    ext_rows = BF16_ROWS + L
    shift_k = -(-SSD_CONV * ext_rows // LANES) * LANES
    st = lax.broadcasted_iota(jnp.int32, (L, shift_k), 0)
    sc = lax.broadcasted_iota(jnp.int32, (L, shift_k), 1)
    hit = sc < 0
    for k in range(SSD_CONV):
        hit = hit | (sc == st + (k * ext_rows + BF16_ROWS - (SSD_CONV - 1 - k)))
    shift_mat = hit.astype(BF16)
    a_row = -jnp.exp(alog_ref[...]) * LOG2_E

    def chunk(ci, carry):
        rows = pl.ds(pl.multiple_of(ci * L, L), L)

        cblk = 512
        for c0 in range(0, SSD_CONV_DIM, cblk):
            cols = slice(c0, c0 + cblk)
            ext = jnp.concatenate([tail_ref[:, cols], xbc_ref[rows, cols]], axis=0)
            ext = ext.reshape(ext_rows // BF16_ROWS, BF16_ROWS, cblk)
            prods = [(ext * cwb_ref[k * BF16_ROWS:(k + 1) * BF16_ROWS, cols][None]
                      ).reshape(ext_rows, cblk) for k in range(SSD_CONV)]
            prods.append(jnp.zeros((shift_k - SSD_CONV * ext_rows, cblk), BF16))
            acc = _dot(shift_mat, jnp.concatenate(prods, axis=0)) + cb_ref[:, cols]
            tail_ref[:, cols] = xbc_ref[pl.ds(pl.multiple_of(ci * L + L - BF16_ROWS, BF16_ROWS),
                                              BF16_ROWS), cols]
            xcv_ref[:, cols] = acc * _sigmoid(acc)

        dtc = dt_ref[rows, :]
        acum = _dot(tri2, jnp.concatenate(_split_bf16(dtc * a_row), axis=0))
        acum_t = acum.T
        both = jnp.concatenate(_split_bf16(jnp.concatenate([dtc, acum], axis=0)), axis=1)
        ex = _dot(both, expand2)
        dt_exp = ex[:L]
        ac_exp = ex[L:]
        a_last = ac_exp[L - 1:L, :]
        xdt = xcv_ref[:, :SSD_INNER] * dt_exp
        xdt_ref[...] = xdt.astype(BF16)
        xw_ref[...] = (xdt * jnp.exp2(a_last - ac_exp)).astype(BF16)
        eac_ref[...] = jnp.exp2(ac_exp)
        chunk_decay = jnp.exp2(a_last)

        for g in range(SSD_GROUPS):
            gcols = slice(g * GROUP_WIDTH, (g + 1) * GROUP_WIDTH)
            b0 = SSD_INNER + g * SSD_STATE
            c0 = SSD_INNER + SSD_GROUPS * SSD_STATE + g * SSD_STATE
            bm_f = xcv_ref[:, b0:b0 + SSD_STATE]
            bm_b = bm_f.astype(BF16)
            bmt_b = bm_f.T.astype(BF16)
            cm_b = xcv_ref[:, c0:c0 + SSD_STATE].astype(BF16)
            cbm = lax.dot_general(cm_b, bm_b, (((1,), (1,)), ((), ())),
                                  preferred_element_type=F32)
            st = state_ref[g]
            y_off = _dot(cm_b, st.astype(BF16)) * eac_ref[:, gcols]
            state_ref[g] = st * chunk_decay[:, gcols] + _dot(bmt_b, xw_ref[:, gcols])

            y_pairs = []
            for jj in range(2):
                j = 2 * g + jj
                scores = []
                for h in (2 * j, 2 * j + 1):
                    seg = acum[:, h:h + 1] - acum_t[h:h + 1, :]
                    decay = jnp.exp2(jnp.where(causal, seg, -jnp.inf))
                    scores.append((cbm * decay).astype(BF16))
                lhs = jnp.concatenate(scores, axis=1)
                xp = xdt_ref[:, j * LANES:(j + 1) * LANES]
                zero = jnp.zeros_like(xp)
                rhs = jnp.concatenate([jnp.where(low_half, xp, zero),
                                       jnp.where(low_half, zero, xp)], axis=0)
                y_pairs.append(_dot(lhs, rhs))
            y = (jnp.concatenate(y_pairs, axis=1) + y_off
                 + xcv_ref[:, gcols] * dskip_ref[:, gcols])
            zf = z_ref[rows, gcols].astype(F32)
            yz = y * (zf * _sigmoid(zf))
            ms = jnp.mean(yz * yz, axis=-1, keepdims=True)
            out_ref[rows, gcols] = (yz * lax.rsqrt(ms + EPS) * nw_ref[:, gcols]).astype(BF16)
        return carry

    lax.fori_loop(0, TS_SSD // L, chunk, 0, unroll=True)


def _ssd(fused, dt, conv_w, conv_b, a_log, d_skip, norm_w, bsz, seq, layer):
    ntok = fused.shape[0]
    nt = seq // TS_SSD
    row = lambda b, s: (b * nt + s, 0)
    return pl.pallas_call(
        _ssd_kernel,
        grid=(bsz, nt),
        in_specs=[
            pl.BlockSpec((TS_SSD, SSD_INNER), lambda b, s: (b * nt + s, Z_BLOCK)),
            pl.BlockSpec((TS_SSD, SSD_CONV_DIM), lambda b, s: (b * nt + s, XBC_BLOCK)),
            pl.BlockSpec((TS_SSD, LANES), row),
            _const_spec((SSD_CONV, SSD_CONV_DIM), layer),
            _const_spec((1, SSD_CONV_DIM), layer),
            _const_spec((1, LANES), layer),
            _const_spec((1, SSD_INNER), layer),
            _const_spec((1, SSD_INNER), layer),
        ],
        out_specs=pl.BlockSpec((TS_SSD, SSD_INNER), row),
        out_shape=jax.ShapeDtypeStruct((ntok, SSD_INNER), BF16),
        scratch_shapes=[
            pltpu.VMEM((SSD_GROUPS, SSD_STATE, GROUP_WIDTH), F32),
            pltpu.VMEM((BF16_ROWS, SSD_CONV_DIM), BF16),
            pltpu.VMEM((SSD_CONV * BF16_ROWS, SSD_CONV_DIM), BF16),
            pltpu.VMEM((SSD_CHUNK, SSD_CONV_DIM), F32),
            pltpu.VMEM((SSD_CHUNK, SSD_INNER), BF16),
            pltpu.VMEM((SSD_CHUNK, SSD_INNER), BF16),
            pltpu.VMEM((SSD_CHUNK, SSD_INNER), F32),
        ],
        compiler_params=pltpu.CompilerParams(
            dimension_semantics=("arbitrary", "arbitrary"), vmem_limit_bytes=VMEM_LIMIT),
        name="ssd",
    )(fused, fused, dt, conv_w, conv_b, a_log, d_skip, norm_w)


def _merge_kernel(yt_ref, ya_ref, gs_ref, bg_ref, x_ref, cb_ref, lng_ref, lnb_ref,
                  wa_ref, wb_ref, wo_ref, out_ref):
    c = jnp.concatenate([yt_ref[:, j, :].T for j in range(TM_MERGE // CONV_BLOCK)], axis=0)
    c = c + cb_ref[...]
    mu = jnp.mean(c, axis=-1, keepdims=True)
    d = c - mu
    var = jnp.mean(d * d, axis=-1, keepdims=True)
    yb = d * lax.rsqrt(var + EPS) * lng_ref[...] + lnb_ref[...]
    yb = (yb * _sigmoid(yb)).astype(BF16)
    pa = _dot(ya_ref[...], wa_ref[...])
    pb = _dot(yb, wb_ref[...])
    ga = _sigmoid(gs_ref[:, :D_MODEL].astype(F32) + bg_ref[:, :D_MODEL])
    gb = _sigmoid(gs_ref[:, D_MODEL:].astype(F32) + bg_ref[:, D_MODEL:])
    merged = ga * pa + gb * pb
    out_ref[...] = x_ref[...] + _dot(merged.astype(BF16), wo_ref[...])


def _merge(yt, ya, fused, b_gate, x2d, conv_b, ln_g, ln_b, w_a, w_b, w_o, layer):
    ntok = ya.shape[0]
    row = lambda i: (i, 0)
    return pl.pallas_call(
        _merge_kernel,
        grid=(ntok // TM_MERGE,),
        in_specs=[
            pl.BlockSpec((CM_DIM, TM_MERGE // CONV_BLOCK, CONV_BLOCK), lambda i: (0, i, 0)),
            pl.BlockSpec((TM_MERGE, SSD_INNER), row),
            pl.BlockSpec((TM_MERGE, 2 * D_MODEL), lambda i: (i, GATES_BLOCK)),
            _const_spec((1, 2 * D_MODEL), layer),
            pl.BlockSpec((TM_MERGE, D_MODEL), row),
            _const_spec((1, CM_DIM), layer),
            _const_spec((1, CM_DIM), layer),
            _const_spec((1, CM_DIM), layer),
            _const_spec((SSD_INNER, D_MODEL), layer),
            _const_spec((CM_DIM, D_MODEL), layer),
            _const_spec((D_MODEL, D_MODEL), layer),
        ],
        out_specs=pl.BlockSpec((TM_MERGE, D_MODEL), row),
        out_shape=jax.ShapeDtypeStruct((ntok, D_MODEL), F32),
        compiler_params=pltpu.CompilerParams(
            dimension_semantics=("arbitrary",), vmem_limit_bytes=VMEM_LIMIT),
        name="merge",
    )(yt, ya, fused, b_gate, x2d, conv_b, ln_g, ln_b, w_a, w_b, w_o)


def _ffn_kernel(x_ref, p_ref, gf_ref, wup_ref, fcw_ref, fcb_ref, wdn_ref, gp_ref, wpg_ref,
                wpe_ref, gfin_ref, out_ref, h_ref, tail_ref, acc_ref, *, final):
    @pl.when(pl.program_id(1) == 0)
    def _():
        tail_ref[...] = jnp.zeros_like(tail_ref)

    xf = x_ref[...]
    h_ref[...] = _rms(xf, gf_ref[...]).astype(BF16)
    acc_ref[...] = xf

    def up_cols(c, half):
        return slice(half * FFN_DIM + c * CW_FFN, half * FFN_DIM + (c + 1) * CW_FFN)

    def up(c):
        return [_dot(h_ref[...], wup_ref[:, up_cols(c, half)]) for half in range(2)]

    n_chunks = FFN_DIM // CW_FFN
    pending = [up(c) for c in range(FFN_LOOKAHEAD)]
    for c in range(n_chunks):
        if c + FFN_LOOKAHEAD < n_chunks:
            pending.append(up(c + FFN_LOOKAHEAD))
        current = pending.pop(0)
        halves = []
        for half, u in enumerate(current):
            cols = up_cols(c, half)
            ext = jnp.concatenate([tail_ref[:, cols], u], axis=0)
            y = fcb_ref[:, cols] + fcw_ref[FFN_CONV - 1:FFN_CONV, cols] * u
            for sft in range(1, FFN_CONV):
                k = FFN_CONV - 1 - sft
                y = y + fcw_ref[k:k + 1, cols] * _shift_rows(ext, sft)
            tail_ref[:, cols] = u[TS_FFN - SUBLANES:, :]
            halves.append(y)
        gate, val = halves
        act = (gate * _sigmoid(gate) * val).astype(BF16)
        acc_ref[...] += _dot(act, wdn_ref[c * CW_FFN:(c + 1) * CW_FFN, :])
    x1 = acc_ref[...]
    h2 = _rms(x1, gp_ref[...]).astype(BF16)
    pg = _sigmoid(_dot(h2, wpg_ref[...]))
    pe = _dot(p_ref[...].astype(BF16), wpe_ref[...])
    x2 = x1 + pg * pe
    if final:
        x2 = _rms(x2, gfin_ref[...])
    out_ref[...] = x2


def _ffn(x2d, p3d, g_ffn, w_up, conv_w, conv_b, w_down, g_ple, w_pg, w_pe, g_final,
         bsz, seq, layer, final):
    ntok = x2d.shape[0]
    nt = seq // TS_FFN
    row = lambda b, s: (b * nt + s, 0)
    return pl.pallas_call(
        functools.partial(_ffn_kernel, final=final),
        grid=(bsz, nt),
        in_specs=[
            pl.BlockSpec((TS_FFN, D_MODEL), row),
            pl.BlockSpec((None, TS_FFN, PLE_DIM), lambda b, s: (layer, b * nt + s, 0)),
            _const_spec((1, D_MODEL), layer),
            _const_spec((D_MODEL, 2 * FFN_DIM), layer),
            _const_spec((FFN_CONV, 2 * FFN_DIM), layer),
            _const_spec((1, 2 * FFN_DIM), layer),
            _const_spec((FFN_DIM, D_MODEL), layer),
            _const_spec((1, D_MODEL), layer),
            _const_spec((D_MODEL, D_MODEL), layer),
            _const_spec((PLE_DIM, D_MODEL), layer),
            _const_spec((1, D_MODEL)),
        ],
        out_specs=pl.BlockSpec((TS_FFN, D_MODEL), row),
        out_shape=jax.ShapeDtypeStruct((ntok, D_MODEL), F32),
        scratch_shapes=[
            pltpu.VMEM((TS_FFN, D_MODEL), BF16),
            pltpu.VMEM((SUBLANES, 2 * FFN_DIM), F32),
            pltpu.VMEM((TS_FFN, D_MODEL), F32),
        ],
        compiler_params=pltpu.CompilerParams(
            dimension_semantics=("arbitrary", "arbitrary"), vmem_limit_bytes=VMEM_LIMIT),
        name="ffn_ple",
    )(x2d, p3d, g_ffn, w_up, conv_w, conv_b, w_down, g_ple, w_pg, w_pe, g_final)


def _prep_w_in_kernel(wt_ref, wall_ref, wdt_ref):
    dt0 = SSD_INNER + SSD_CONV_DIM
    val0 = dt0 + SSD_HEADS
    gs0 = val0 + 2 * CM_DIM
    pieces = ((val0, gs0), (SSD_INNER, dt0), (gs0, gs0 + 2 * D_MODEL), (0, SSD_INNER))
    at = 0
    for lo, hi in pieces:
        wall_ref[:, at:at + hi - lo] = wt_ref[lo:hi, :].T.astype(BF16)
        at += hi - lo
    lane = lax.broadcasted_iota(jnp.int32, (PREP_ROWS, LANES), 1)
    wdt_ref[...] = jnp.where(lane < SSD_HEADS, wt_ref[dt0:dt0 + LANES, :].T, 0.0).astype(BF16)


def _prep_w_in(w_in):
    nl, d, ncol = w_in.shape
    out_spec = lambda width: pl.BlockSpec((None, PREP_ROWS, width), lambda l, i: (l, i, 0))
    return pl.pallas_call(
        _prep_w_in_kernel,
        grid=(nl, d // PREP_ROWS),
        in_specs=[pl.BlockSpec((None, ncol, PREP_ROWS), lambda l, i: (l, 0, i))],
        out_specs=[out_spec(2 * CM_DIM + FUSED_DIM), out_spec(LANES)],
        out_shape=[jax.ShapeDtypeStruct((nl, d, 2 * CM_DIM + FUSED_DIM), BF16),
                   jax.ShapeDtypeStruct((nl, d, LANES), BF16)],
        compiler_params=pltpu.CompilerParams(
            dimension_semantics=("arbitrary", "arbitrary"), vmem_limit_bytes=VMEM_LIMIT),
        name="prep_w_in",
    )(jnp.swapaxes(w_in, 1, 2))


def _pad_lanes(a, width):
    return jnp.pad(a, [(0, 0)] * (a.ndim - 1) + [(0, width - a.shape[-1])])


def _rows(a):
    return a[:, None, :]


def kernel(x, p, norm_mix, w_in, b_gate, ssd_conv_w, ssd_conv_b, dt_bias, a_log, d_skip, ssd_norm, w_ssd_out, cm_conv_w, cm_conv_b, cm_ln_g, cm_ln_b, w_cm_out, w_out, norm_ffn, w_up, ffn_conv_w, ffn_conv_b, w_down, norm_ple, w_ple_gate, w_ple_proj, final_norm):
    bsz, seq, d = x.shape
    depth = w_in.shape[0]
    ntok = bsz * seq
    w_all, w_dt = _prep_w_in(w_in)
    tw = _toeplitz(_pad_lanes(jnp.swapaxes(cm_conv_w, 1, 2), LANES).astype(BF16))
    w_ssd_out_b, w_cm_out_b, w_out_b = _to_bf16(w_ssd_out), _to_bf16(w_cm_out), _to_bf16(w_out)
    w_up_b, w_down_b = _to_bf16(w_up), _to_bf16(w_down)
    w_pg_b, w_pe_b = _to_bf16(w_ple_gate), _to_bf16(w_ple_proj)
    norm_mix_r, dt_bias_r = _rows(norm_mix), _rows(_pad_lanes(dt_bias, LANES))
    ssd_conv_b_r, a_log_r = _rows(ssd_conv_b), _rows(_pad_lanes(a_log, LANES))
    d_skip_r, ssd_norm_r = _rows(jnp.repeat(d_skip, SSD_HEAD_DIM, axis=1)), _rows(ssd_norm)
    b_gate_r = b_gate.reshape(depth, 1, -1)
    cm_conv_b_r, cm_ln_g_r, cm_ln_b_r = _rows(cm_conv_b), _rows(cm_ln_g), _rows(cm_ln_b)
    norm_ffn_r, ffn_conv_b_r, norm_ple_r = _rows(norm_ffn), _rows(ffn_conv_b), _rows(norm_ple)
    p3d = p.reshape(depth, ntok, PLE_DIM)

    x2d = x.reshape(ntok, d)
    for i in range(depth):
        fused, vt, dt = _inproj(x2d, norm_mix_r, w_all, w_dt, dt_bias_r, i)
        yt = _conv31(vt, tw, seq, i)
        ya = _ssd(fused, dt, ssd_conv_w, ssd_conv_b_r, a_log_r, d_skip_r, ssd_norm_r, bsz, seq, i)
        x2d = _merge(yt, ya, fused, b_gate_r, x2d, cm_conv_b_r, cm_ln_g_r, cm_ln_b_r,
                     w_ssd_out_b, w_cm_out_b, w_out_b, i)
        x2d = _ffn(x2d, p3d, norm_ffn_r, w_up_b, ffn_conv_w, ffn_conv_b_r, w_down_b, norm_ple_r,
                   w_pg_b, w_pe_b, final_norm[None], bsz, seq, i, final=(i == depth - 1))
    return x2d.reshape(bsz, seq, d)
```
